```python
import jax, jax.numpy as jnp
from jax import lax
import numpy as np

D_MODEL = 1024
BATCH = 4
SEQ = 8192
DEPTH = 1

POOL_WINDOWS = (2, 4, 8, 16)
POOL_GROUPS = len(POOL_WINDOWS)
POOL_GROUP_DIM = D_MODEL // 16
POOL_WIDTH = POOL_GROUPS * POOL_GROUP_DIM
ATTN_PATTERNS = ((128, 1), (512, 4), (2048, 16))
N_ATTN_GROUPS = len(ATTN_PATTERNS)
HEAD_DIM = 64
HEADS_PER_GROUP = 4
N_ATTN_HEADS = N_ATTN_GROUPS * HEADS_PER_GROUP
ATTN_WIDTH = N_ATTN_HEADS * HEAD_DIM
ATTN_OUT_WIDTH = HEADS_PER_GROUP * HEAD_DIM
ROPE_THETA = 10000.0
BLOCK = 128
N_BRANCHES = 2
IN_WIDTH = POOL_WIDTH + 3 * ATTN_WIDTH + N_BRANCHES * D_MODEL
D_FF = ((8 * D_MODEL // 3 + 255) // 256) * 256
N_MOD = 9
EPS = 1e-6

kernel_name = "hybrid_pool_dilated_attn_macaron_block"


def rmsnorm(x, g):
    xf = x.astype(jnp.float32)
    y = xf * lax.rsqrt(jnp.mean(xf * xf, axis=-1, keepdims=True) + EPS)
    return (y * g.astype(jnp.float32)).astype(x.dtype)


def modulate(x, shift, scale):
    return x * (1 + scale) + shift


def swiglu(u, w_in, w_out):
    a, b = jnp.split(u @ w_in, 2, axis=-1)
    return (jax.nn.silu(a) * b) @ w_out


def rope_tables(positions, dtype):
    inv_freq = ROPE_THETA ** (-jnp.arange(0, HEAD_DIM, 2, dtype=jnp.float32) / HEAD_DIM)
    ang = positions.astype(jnp.float32)[..., None] * inv_freq
    return jnp.cos(ang)[:, :, None, :].astype(dtype), jnp.sin(ang)[:, :, None, :].astype(dtype)


def apply_rope(t, cos, sin):
    t1, t2 = jnp.split(t, 2, axis=-1)
    return jnp.concatenate([t1 * cos - t2 * sin, t2 * cos + t1 * sin], axis=-1)


def multiscale_pool(p, w_pool, pool_scale):
    B, S, _ = p.shape
    pf = p.astype(jnp.float32)
    cs = jnp.pad(jnp.cumsum(pf, axis=1), ((0, 0), (1, 0), (0, 0)))
    t = jnp.arange(S)
    outs = []
    for gi, w in enumerate(POOL_WINDOWS):
        sl = slice(gi * POOL_GROUP_DIM, (gi + 1) * POOL_GROUP_DIM)
        csg = cs[..., sl]
        lagged = jnp.pad(csg[:, :S + 1 - w], ((0, 0), (w - 1, 0), (0, 0)))
        count = jnp.minimum(t + 1, w).astype(jnp.float32)[None, :, None]
        outs.append((csg[:, 1:] - lagged) / count - pf[..., sl])
    d = jnp.stack(outs, axis=2).astype(p.dtype)
    y = jnp.einsum('bsgc,gcd->bsgd', d, w_pool)
    return y.reshape(B, S, POOL_WIDTH) * pool_scale


def dilated_window_attention(q, k, v, steps, dilation):
    B, S, H, Dh = q.shape
    L = S // dilation
    nb = -(-L // BLOCK)
    Lp = nb * BLOCK

    def to_strided(t):
        t = t.reshape(B, L, dilation, H, Dh).transpose(0, 3, 2, 1, 4)
        return jnp.pad(t, ((0, 0), (0, 0), (0, 0), (0, Lp - L), (0, 0)))

    qs, ks, vs = to_strided(q), to_strided(k), to_strided(v)
    qb = qs.reshape(B, H, dilation, nb, BLOCK, Dh)

    def band(t):
        tp = jnp.pad(t, ((0, 0), (0, 0), (0, 0), (BLOCK, 0), (0, 0)))
        prev = tp[..., :Lp, :].reshape(B, H, dilation, nb, BLOCK, Dh)
        cur = t.reshape(B, H, dilation, nb, BLOCK, Dh)
        return jnp.concatenate([prev, cur], axis=-2)

    kb, vb = band(ks), band(vs)
    a = jnp.arange(BLOCK)[:, None]
    cidx = jnp.arange(2 * BLOCK)[None, :]
    rel = a + BLOCK - cidx
    in_band = (rel >= 0) & (rel <= steps)
    key_pos = jnp.arange(nb)[:, None, None] * BLOCK - BLOCK + cidx[None]
    mask = in_band[None] & (key_pos >= 0)

    s = jnp.einsum('bhrnqd,bhrnkd->bhrnqk', qb, kb).astype(jnp.float32) * (HEAD_DIM ** -0.5)
    s = jnp.where(mask, s, -jnp.inf)
    lse = jax.nn.logsumexp(s, axis=-1)
    pr = jnp.exp(s - lse[..., None]).astype(v.dtype)
    o = jnp.einsum('bhrnqk,bhrnkd->bhrnqd', pr, vb)
    o = o.reshape(B, H, dilation, Lp, Dh)[:, :, :, :L]
    o = o.transpose(0, 3, 2, 1, 4).reshape(B, S, H, Dh)
    lse = lse.reshape(B, H, dilation, Lp)[:, :, :, :L].transpose(0, 3, 2, 1).reshape(B, S, H)
    return o, lse


def token_mixing(u, cos, sin, w_in, w_pool, pool_scale, w_pool_branch, w_attn_branch, w_out):
    B, S, _ = u.shape
    proj = u @ w_in
    cuts = [POOL_WIDTH, POOL_WIDTH + ATTN_WIDTH, POOL_WIDTH + 2 * ATTN_WIDTH,
            POOL_WIDTH + 3 * ATTN_WIDTH]
    p, q, k, v, gate_logits = jnp.split(proj, cuts, axis=-1)

    y_pool = multiscale_pool(p, w_pool, pool_scale)

    q = apply_rope(q.reshape(B, S, N_ATTN_HEADS, HEAD_DIM), cos, sin)
    k = apply_rope(k.reshape(B, S, N_ATTN_HEADS, HEAD_DIM), cos, sin)
    v = v.reshape(B, S, N_ATTN_HEADS, HEAD_DIM)
    outs, lses = [], []
    for gi, (window, dilation) in enumerate(ATTN_PATTERNS):
        hs = slice(gi * HEADS_PER_GROUP, (gi + 1) * HEADS_PER_GROUP)
        o, lse = dilated_window_attention(q[:, :, hs], k[:, :, hs], v[:, :, hs],
                                          window // dilation, dilation)
        outs.append(o)
        lses.append(lse)
    wts = jax.nn.softmax(jnp.stack(lses, axis=0), axis=0)
    y_attn = jnp.einsum('gbsh,gbshd->bshd', wts.astype(v.dtype), jnp.stack(outs, axis=0))
    y_attn = y_attn.reshape(B, S, ATTN_OUT_WIDTH)

    gates = jax.nn.sigmoid(gate_logits.astype(jnp.float32)).astype(u.dtype)
    g_pool, g_attn = jnp.split(gates, N_BRANCHES, axis=-1)
    merged = g_pool * (y_pool @ w_pool_branch) + g_attn * (y_attn @ w_attn_branch)
    return merged @ w_out


def setup_inputs(seed: int = 0) -> dict:
    key = jax.random.key(seed)
    ks = jax.random.split(key, 20)
    f32 = jnp.float32

    def lin(k, shape, fan_in, scale=1.0):
        return jax.random.normal(k, shape, f32) * (scale * fan_in ** -0.5)

    def gain(k, shape):
        return 1.0 + 0.05 * jax.random.normal(k, shape, f32)

    x = jax.random.normal(ks[0], (BATCH, SEQ, D_MODEL), f32)
    c = jax.random.normal(ks[1], (BATCH, D_MODEL), f32)
    offset = jax.random.randint(ks[2], (BATCH, 1), 0, 1024, dtype=jnp.int32)
    positions = (jnp.arange(SEQ, dtype=jnp.int32)[None, :] + offset).astype(jnp.int32)
    return {
        "x": x,
        "c": c,
        "positions": positions,
        "w_ada": lin(ks[3], (DEPTH, D_MODEL, N_MOD * D_MODEL), D_MODEL, 0.5),
        "b_ada": 0.02 * jax.random.normal(ks[4], (DEPTH, N_MOD * D_MODEL), f32),
        "g_norm_ffn1": gain(ks[5], (DEPTH, D_MODEL)),
        "w_ffn1_in": lin(ks[6], (DEPTH, D_MODEL, 2 * D_FF), D_MODEL),
        "w_ffn1_out": lin(ks[7], (DEPTH, D_FF, D_MODEL), D_FF),
        "g_norm_mix": gain(ks[8], (DEPTH, D_MODEL)),
        "w_in": lin(ks[9], (DEPTH, D_MODEL, IN_WIDTH), D_MODEL),
        "w_pool": lin(ks[10], (DEPTH, POOL_GROUPS, POOL_GROUP_DIM, POOL_GROUP_DIM), POOL_GROUP_DIM),
        "pool_scale": 1.0 + 0.1 * jax.random.normal(ks[11], (DEPTH, POOL_WIDTH), f32),
        "w_pool_branch": lin(ks[12], (DEPTH, POOL_WIDTH, D_MODEL), POOL_WIDTH),
        "w_attn_branch": lin(ks[13], (DEPTH, ATTN_OUT_WIDTH, D_MODEL), ATTN_OUT_WIDTH),
        "w_out": lin(ks[14], (DEPTH, D_MODEL, D_MODEL), D_MODEL),
        "g_norm_ffn2": gain(ks[15], (DEPTH, D_MODEL)),
        "w_ffn2_in": lin(ks[16], (DEPTH, D_MODEL, 2 * D_FF), D_MODEL),
        "w_ffn2_out": lin(ks[17], (DEPTH, D_FF, D_MODEL), D_FF),
        "g_final": gain(ks[18], (D_MODEL,)),
    }


def reference(x, c, positions, w_ada, b_ada, g_norm_ffn1, w_ffn1_in, w_ffn1_out,
              g_norm_mix, w_in, w_pool, pool_scale, w_pool_branch, w_attn_branch, w_out,
              g_norm_ffn2, w_ffn2_in, w_ffn2_out, g_final):
    cos, sin = rope_tables(positions, x.dtype)
    cond = jax.nn.silu(c)
    h = x
    for l in range(DEPTH):
        mod = cond @ w_ada[l] + b_ada[l]
        sh1, sc1, gt1, sh2, sc2, gt2, sh3, sc3, gt3 = [
            m[:, None, :] for m in jnp.split(mod, N_MOD, axis=-1)]
        u = modulate(rmsnorm(h, g_norm_ffn1[l]), sh1, sc1)
        h = h + 0.5 * gt1 * swiglu(u, w_ffn1_in[l], w_ffn1_out[l])
        u = modulate(rmsnorm(h, g_norm_mix[l]), sh2, sc2)
        h = h + gt2 * token_mixing(u, cos, sin, w_in[l], w_pool[l], pool_scale[l],
                                   w_pool_branch[l], w_attn_branch[l], w_out[l])
        u = modulate(rmsnorm(h, g_norm_ffn2[l]), sh3, sc3)
        h = h + 0.5 * gt3 * swiglu(u, w_ffn2_in[l], w_ffn2_out[l])
    return rmsnorm(h, g_final)
```

```python
import functools

import jax
import jax.numpy as jnp
from jax import lax
from jax.experimental import pallas as pl
from jax.experimental.pallas import tpu as pltpu

F32 = jnp.float32
BF16 = jnp.bfloat16

D_MODEL = 1024
POOL_WINDOWS = (2, 4, 8, 16)
POOL_GROUP_DIM = 64
POOL_WIDTH = 256
ATTN_DILATIONS = (1, 4, 16)
ATTN_STEPS = 128
N_GROUPS = 3
HEAD_DIM = 64
GROUP_WIDTH = 256
ATTN_WIDTH = N_GROUPS * GROUP_WIDTH
PQKV_WIDTH = POOL_WIDTH + 3 * ATTN_WIDTH
D_FF = 2816
N_MOD = 9
ROPE_THETA = 10000.0
EPS = 1e-6
NEG_BIG = -1e30

LANES = 128
VMEM_LIMIT_BYTES = 56 * 1024 * 1024

TOKEN_TILE = 256
ATTN_TILE = 2048
QBLK = 128
POOL_HALO = 16


def _resident(shape):
    nd = len(shape)
    return pl.BlockSpec(shape, lambda *_: (0,) * nd, pipeline_mode=pl.Buffered(1))


def _rms_modulate(x, g, shift, scale):
    ms = jnp.mean(x * x, axis=-1, keepdims=True)
    y = x * lax.rsqrt(ms + EPS) * g
    return y * (1.0 + scale) + shift


def _swiglu(u_bf16, w_in_ref, w_out_ref):
    hcat = jnp.dot(u_bf16, w_in_ref[...], preferred_element_type=F32)
    a = hcat[:, :D_FF]
    b = hcat[:, D_FF:]
    act = (a * jax.nn.sigmoid(a) * b).astype(BF16)
    return jnp.dot(act, w_out_ref[...], preferred_element_type=F32)


def _ada_kernel(c_ref, w_ref, b_ref, o_ref):
    c = c_ref[...]
    cond = c * jax.nn.sigmoid(c)
    c_hi = cond.astype(BF16)
    c_lo = (cond - c_hi.astype(F32)).astype(BF16)
    w = w_ref[...]
    w_hi = w.astype(BF16)
    w_lo = (w - w_hi.astype(F32)).astype(BF16)
    acc = jnp.dot(c_hi, w_hi, preferred_element_type=F32)
    acc += jnp.dot(c_hi, w_lo, preferred_element_type=F32)
    acc += jnp.dot(c_lo, w_hi, preferred_element_type=F32)
    o_ref[...] = acc + b_ref[...]


def _ada_modulation(c, w_ada, b_ada):
    batch = c.shape[0]
    return pl.pallas_call(
        _ada_kernel,
        grid=(N_MOD,),
        in_specs=[
            pl.BlockSpec((batch, D_MODEL), lambda j: (0, 0)),
            pl.BlockSpec((D_MODEL, D_MODEL), lambda j: (0, j)),
            pl.BlockSpec((1, D_MODEL), lambda j: (0, j)),
        ],
        out_specs=pl.BlockSpec((batch, D_MODEL), lambda j: (0, j)),
        out_shape=jax.ShapeDtypeStruct((batch, N_MOD * D_MODEL), F32),
        compiler_params=pltpu.CompilerParams(dimension_semantics=("parallel",)),
        name="ada_modulation",
    )(c, w_ada, b_ada.reshape(1, N_MOD * D_MODEL))


def _ffn1_proj_kernel(x_ref, mod_ref, g1_ref, g2_ref, pos_ref, invf_ref,
                      w1i_ref, w1o_ref, wp_ref,
                      h_ref, p_ref, q_ref, k_ref, v_ref):
    tm = x_ref.shape[1]
    x = x_ref[0]
    mod = mod_ref[0]
    u = _rms_modulate(x, g1_ref[...], mod[0:1], mod[1:2]).astype(BF16)
    h = x + (0.5 * mod[2:3]) * _swiglu(u, w1i_ref, w1o_ref)
    h_ref[0] = h

    u2 = _rms_modulate(h, g2_ref[...], mod[3:4], mod[4:5]).astype(BF16)
    proj = jnp.dot(u2, wp_ref[...], preferred_element_type=F32)
    p_ref[0] = proj[:, :POOL_WIDTH]

    lane = lax.broadcasted_iota(jnp.int32, (QBLK, LANES), 1)
    first_half = (lane & (HEAD_DIM - 1)) < (HEAD_DIM // 2)
    invf = invf_ref[...]
    for c in range(tm // QBLK):
        rows = slice(c * QBLK, (c + 1) * QBLK)
        pos_row = pos_ref[0, 0, c:c + 1, :].astype(F32)
        pos_col = jnp.broadcast_to(pos_row, (QBLK, LANES)).T
        ang = pos_col * invf
        cosf = jnp.cos(ang)
        sinf = jnp.sin(ang)
        sin_signed = jnp.where(first_half, -sinf, sinf)
        for g in range(N_GROUPS):
            for sl in range(GROUP_WIDTH // LANES):
                off = g * GROUP_WIDTH + sl * LANES
                cols = slice(sl * LANES, (sl + 1) * LANES)
                for base, ref, scale in ((POOL_WIDTH, q_ref, HEAD_DIM ** -0.5),
                                         (POOL_WIDTH + ATTN_WIDTH, k_ref, None)):
                    t = proj[rows, base + off:base + off + LANES]
                    rot = jnp.where(first_half, pltpu.roll(t, LANES - HEAD_DIM // 2, 1),
                                    pltpu.roll(t, HEAD_DIM // 2, 1))
                    r = t * cosf + rot * sin_signed
                    if scale is not None:
                        r = r * scale
                    ref[g, 0, rows, cols] = r.astype(BF16)
    vbase = POOL_WIDTH + 2 * ATTN_WIDTH
    for g in range(N_GROUPS):
        v_ref[g, 0] = proj[:, vbase + g * GROUP_WIDTH:vbase + (g + 1) * GROUP_WIDTH].astype(BF16)


def _ffn1_proj(x, mod, g1, g2, pos4, invf, w1i, w1o, wp):
    batch, seq, _ = x.shape
    tm = TOKEN_TILE
    grid = (batch, seq // tm)
    tile = lambda w: pl.BlockSpec((1, tm, w), lambda b, i: (b, i, 0))
    grp = pl.BlockSpec((N_GROUPS, 1, tm, GROUP_WIDTH), lambda b, i: (0, b, i, 0))
    grp_shape = jax.ShapeDtypeStruct((N_GROUPS, batch, seq, GROUP_WIDTH), BF16)
    return pl.pallas_call(
        _ffn1_proj_kernel,
        grid=grid,
        in_specs=[
            tile(D_MODEL),
            pl.BlockSpec((1, N_MOD, D_MODEL), lambda b, i: (b, 0, 0)),
            _resident((1, D_MODEL)),
            _resident((1, D_MODEL)),
            pl.BlockSpec((1, 1, tm // LANES, LANES), lambda b, i: (b, i, 0, 0)),
            _resident((1, LANES)),
            _resident(w1i.shape),
            _resident(w1o.shape),
            _resident(wp.shape),
        ],
        out_specs=[tile(D_MODEL), tile(POOL_WIDTH), grp, grp, grp],
        out_shape=[
            jax.ShapeDtypeStruct((batch, seq, D_MODEL), F32),
            jax.ShapeDtypeStruct((batch, seq, POOL_WIDTH), F32),
            grp_shape, grp_shape, grp_shape,
        ],
        compiler_params=pltpu.CompilerParams(
            dimension_semantics=("parallel", "parallel"),
            vmem_limit_bytes=VMEM_LIMIT_BYTES),
        name="ffn1_proj",
    )(x, mod, g1, g2, pos4, invf, w1i, w1o, wp)


def _attn_kernel(q_ref, kc_ref, kp_ref, vc_ref, vp_ref, o_ref, lse_ref, *, dilation):
    n = q_ref.shape[2]
    nq = n // QBLK
    has_prev = pl.program_id(1) > 0

    a = lax.broadcasted_iota(jnp.int32, (QBLK, QBLK), 0)
    c = lax.broadcasted_iota(jnp.int32, (QBLK, QBLK), 1)
    bias_prev = jnp.where(c >= a, 0.0, NEG_BIG).astype(F32)
    bias_cur = jnp.where(c <= a, 0.0, NEG_BIG).astype(F32)
    bias_first = jnp.where(has_prev, bias_prev, NEG_BIG)
    lane = lax.broadcasted_iota(jnp.int32, (QBLK, LANES), 1)
    head0 = lane < HEAD_DIM

    def slab(col0, qs, k_prev, k_cur, v_prev, v_cur, b_prev):
        outs, lses = [], []
        for hh in range(2):
            sel = head0 if hh == 0 else jnp.logical_not(head0)
            qm = jnp.where(sel, qs, jnp.zeros_like(qs))
            dn = (((1,), (1,)), ((), ()))
            s_p = lax.dot_general(qm, k_prev, dn, preferred_element_type=F32) + b_prev
            s_c = lax.dot_general(qm, k_cur, dn, preferred_element_type=F32) + bias_cur
            m = jnp.max(jnp.maximum(s_p, s_c), axis=-1, keepdims=True)
            e_p = jnp.exp(s_p - m)
            e_c = jnp.exp(s_c - m)
            l = jnp.sum(e_p + e_c, axis=-1, keepdims=True)
            o = jnp.dot(e_p.astype(BF16), v_prev, preferred_element_type=F32)
            o += jnp.dot(e_c.astype(BF16), v_cur, preferred_element_type=F32)
            outs.append(o / l)
            lses.append(m + jnp.log(l))
        return (jnp.where(head0, outs[0], outs[1]),
                jnp.where(head0, jnp.broadcast_to(lses[0], (QBLK, LANES)),
                          jnp.broadcast_to(lses[1], (QBLK, LANES))))

    for r in range(dilation):
        for sl in range(GROUP_WIDTH // LANES):
            col0 = r * GROUP_WIDTH + sl * LANES
            cols = slice(col0, col0 + LANES)
            first = slice(0, QBLK)
            o, lse = slab(col0, q_ref[0, 0, first, cols], kp_ref[0, 0, :, cols], kc_ref[0, 0, first, cols],
                          vp_ref[0, 0, :, cols], vc_ref[0, 0, first, cols], bias_first)
            o_ref[0, first, cols] = o
            lse_ref[0, first, cols] = lse

            if nq > 1:
                def body(j, carry, cols=cols, col0=col0):
                    cur = pl.ds(pl.multiple_of(j * QBLK, QBLK), QBLK)
                    prev = pl.ds(pl.multiple_of((j - 1) * QBLK, QBLK), QBLK)
                    o, lse = slab(col0, q_ref[0, 0, cur, cols], kc_ref[0, 0, prev, cols], kc_ref[0, 0, cur, cols],
                                  vc_ref[0, 0, prev, cols], vc_ref[0, 0, cur, cols], bias_prev)
                    o_ref[0, cur, cols] = o
                    lse_ref[0, cur, cols] = lse
                    return carry
                lax.fori_loop(1, nq, body, 0)


def _dilated_attention(q, k, v, group):
    d = ATTN_DILATIONS[group]
    _, batch, seq, _ = q.shape
    rows = seq // d
    n = ATTN_TILE // d
    width = d * GROUP_WIDTH
    view = lambda t: t.reshape(N_GROUPS, batch, rows, width)
    cur = pl.BlockSpec((1, 1, n, width), lambda b, i: (group, b, i, 0))
    prev = pl.BlockSpec((1, 1, QBLK, width),
                        lambda b, i: (group, b, jnp.maximum(i * (n // QBLK) - 1, 0), 0))
    out = pl.BlockSpec((1, n, width), lambda b, i: (b, i, 0))
    out_shape = jax.ShapeDtypeStruct((batch, rows, width), F32)
    o, lse = pl.pallas_call(
        functools.partial(_attn_kernel, dilation=d),
        grid=(batch, seq // ATTN_TILE),
        in_specs=[cur, cur, prev, cur, prev],
        out_specs=[out, out],
        out_shape=[out_shape, out_shape],
        compiler_params=pltpu.CompilerParams(
            dimension_semantics=("parallel", "parallel"),
            vmem_limit_bytes=VMEM_LIMIT_BYTES),
        name=f"dilated_attn_d{d}",
    )(view(q), view(k), view(k), view(v), view(v))
    return o.reshape(batch, seq, GROUP_WIDTH), lse.reshape(batch, seq, GROUP_WIDTH)


def _merge_ffn2_kernel(h_ref, p_ref, pprev_ref, o0_ref, o1_ref, o2_ref, l0_ref, l1_ref, l2_ref,
                       mod_ref, gmix_ref, g3_ref, gfin_ref,
                       wg_ref, wpool_ref, pscale_ref, wpb_ref, wab_ref, wout_ref, w2i_ref, w2o_ref,
                       out_ref):
    tm = h_ref.shape[1]
    i = pl.program_id(1)
    h = h_ref[0]
    mod = mod_ref[0]
    u = _rms_modulate(h, gmix_ref[...], mod[3:4], mod[4:5]).astype(BF16)
    gates = jax.nn.sigmoid(jnp.dot(u, wg_ref[...], preferred_element_type=F32))

    p = p_ref[0]
    halo = jnp.where(i > 0, pprev_ref[0], 0.0)
    xs = jnp.concatenate([halo, p], axis=0)
    sums = {1: xs}
    w = 1
    while w < POOL_WINDOWS[-1]:
        s = sums[w]
        sums[2 * w] = s[w:] + s[:-w]
        w *= 2
    lane = lax.broadcasted_iota(jnp.int32, (tm, POOL_WIDTH), 1)
    tpos = lax.broadcasted_iota(jnp.int32, (tm, POOL_WIDTH), 0) + i * tm
    trail = jnp.zeros((tm, POOL_WIDTH), F32)
    win = jnp.zeros((tm, POOL_WIDTH), jnp.int32)
    for gi, wnd in enumerate(POOL_WINDOWS):
        in_group = (lane >> 6) == gi
        start = POOL_HALO - (wnd - 1)
        trail = jnp.where(in_group, sums[wnd][start:start + tm], trail)
        win = jnp.where(in_group, wnd, win)
    count = jnp.minimum(tpos + 1, win).astype(F32)
    dpool = (trail / count - p).astype(BF16)
    y_pool = jnp.dot(dpool, wpool_ref[...], preferred_element_type=F32) * pscale_ref[...]

    l0, l1, l2 = l0_ref[0], l1_ref[0], l2_ref[0]
    mx = jnp.maximum(jnp.maximum(l0, l1), l2)
    e0, e1, e2 = jnp.exp(l0 - mx), jnp.exp(l1 - mx), jnp.exp(l2 - mx)
    y_attn = (e0 * o0_ref[0] + e1 * o1_ref[0] + e2 * o2_ref[0]) / (e0 + e1 + e2)

    merged = gates[:, :D_MODEL] * jnp.dot(y_pool.astype(BF16), wpb_ref[...], preferred_element_type=F32)
    merged += gates[:, D_MODEL:] * jnp.dot(y_attn.astype(BF16), wab_ref[...], preferred_element_type=F32)
    h = h + mod[5:6] * jnp.dot(merged.astype(BF16), wout_ref[...], preferred_element_type=F32)

    u3 = _rms_modulate(h, g3_ref[...], mod[6:7], mod[7:8]).astype(BF16)
    h = h + (0.5 * mod[8:9]) * _swiglu(u3, w2i_ref, w2o_ref)

    ms = jnp.mean(h * h, axis=-1, keepdims=True)
    out_ref[0] = h * lax.rsqrt(ms + EPS) * gfin_ref[...]


def _merge_ffn2(h1, p, o, lse, mod, gmix, g3, gfin, wg, wpool, pscale, wpb, wab, wout, w2i, w2o):
    batch, seq, _ = h1.shape
    tm = TOKEN_TILE
    tile = lambda w: pl.BlockSpec((1, tm, w), lambda b, i: (b, i, 0))
    halo = pl.BlockSpec((1, POOL_HALO, POOL_WIDTH),
                        lambda b, i: (b, jnp.maximum(i * (tm // POOL_HALO) - 1, 0), 0))
    vec = _resident((1, D_MODEL))
    return pl.pallas_call(
        _merge_ffn2_kernel,
        grid=(batch, seq // tm),
        in_specs=[
            tile(D_MODEL), tile(POOL_WIDTH), halo,
            tile(GROUP_WIDTH), tile(GROUP_WIDTH), tile(GROUP_WIDTH),
            tile(GROUP_WIDTH), tile(GROUP_WIDTH), tile(GROUP_WIDTH),
            pl.BlockSpec((1, N_MOD, D_MODEL), lambda b, i: (b, 0, 0)),
            vec, vec, vec,
            _resident(wg.shape), _resident(wpool.shape), _resident(pscale.shape),
            _resident(wpb.shape), _resident(wab.shape), _resident(wout.shape),
            _resident(w2i.shape), _resident(w2o.shape),
        ],
        out_specs=tile(D_MODEL),
        out_shape=jax.ShapeDtypeStruct((batch, seq, D_MODEL), F32),
        compiler_params=pltpu.CompilerParams(
            dimension_semantics=("parallel", "parallel"),
            vmem_limit_bytes=VMEM_LIMIT_BYTES),
        name="merge_ffn2",
    )(h1, p, p, o[0], o[1], o[2], lse[0], lse[1], lse[2], mod, gmix, g3, gfin,
      wg, wpool, pscale, wpb, wab, wout, w2i, w2o)


def kernel(x, c, positions, w_ada, b_ada, g_norm_ffn1, w_ffn1_in, w_ffn1_out, g_norm_mix, w_in, w_pool,
           pool_scale, w_pool_branch, w_attn_branch, w_out, g_norm_ffn2, w_ffn2_in, w_ffn2_out, g_final):
    assert w_ada.shape[0] == 1, "single-layer block"
    batch, seq, d_model = x.shape
    assert d_model == D_MODEL and seq % ATTN_TILE == 0 and seq % TOKEN_TILE == 0

    mod = _ada_modulation(c, w_ada[0], b_ada[0]).reshape(batch, N_MOD, D_MODEL)

    row = lambda g: g.reshape(1, -1)
    bf = lambda w: w.astype(BF16)
    freq = ROPE_THETA ** (-jnp.arange(0, HEAD_DIM, 2, dtype=F32) / HEAD_DIM)
    invf = jnp.tile(freq, LANES // (HEAD_DIM // 2)).reshape(1, LANES)
    pos4 = positions.reshape(batch, seq // TOKEN_TILE, TOKEN_TILE // LANES, LANES)

    w_in0 = w_in[0]
    h1, p, q, k, v = _ffn1_proj(x, mod, row(g_norm_ffn1[0]), row(g_norm_mix[0]), pos4, invf,
                                bf(w_ffn1_in[0]), bf(w_ffn1_out[0]), bf(w_in0[:, :PQKV_WIDTH]))

    o, lse = zip(*[_dilated_attention(q, k, v, g) for g in range(N_GROUPS)])

    w_pool_bd = jnp.zeros((POOL_WIDTH, POOL_WIDTH), F32)
    for gi in range(len(POOL_WINDOWS)):
        sl = slice(gi * POOL_GROUP_DIM, (gi + 1) * POOL_GROUP_DIM)
        w_pool_bd = w_pool_bd.at[sl, sl].set(w_pool[0, gi])

    return _merge_ffn2(h1, p, o, lse, mod, row(g_norm_mix[0]), row(g_norm_ffn2[0]), row(g_final),
                       bf(w_in0[:, PQKV_WIDTH:]), bf(w_pool_bd), row(pool_scale[0]),
                       bf(w_pool_branch[0]), bf(w_attn_branch[0]), bf(w_out[0]),
                       bf(w_ffn2_in[0]), bf(w_ffn2_out[0]))
```

```python
import jax
import jax.numpy as jnp
from jax import lax
from jax.experimental import pallas as pl
from jax.experimental.pallas import tpu as pltpu

F32 = jnp.float32
BF16 = jnp.bfloat16

D_MODEL = 1024
POOL_WINDOWS = (2, 4, 8, 16)
POOL_GROUP_DIM = 64
POOL_WIDTH = 256
ATTN_DILATIONS = (1, 4, 16)
N_GROUPS = 3
HEAD_DIM = 64
GROUP_WIDTH = 256
ATTN_WIDTH = N_GROUPS * GROUP_WIDTH
PQKV_WIDTH = POOL_WIDTH + 3 * ATTN_WIDTH
KVQ_WIDTH = 3 * GROUP_WIDTH
D_FF = 2816
N_MOD = 9
ROPE_THETA = 10000.0
EPS = 1e-6
NEG_BIG = -1e30

LANES = 128
VMEM_LIMIT_BYTES = 56 * 1024 * 1024

TOKEN_TILE = 256
ATTN_ROWS = 512
QBLK = 128
POOL_HALO = 16
SLABS = GROUP_WIDTH // LANES


def _resident(shape):
    nd = len(shape)
    return pl.BlockSpec(shape, lambda *_: (0,) * nd, pipeline_mode=pl.Buffered(1))


def _rms_modulate(x, g, shift, scale):
    ms = jnp.mean(x * x, axis=-1, keepdims=True)
    y = x * lax.rsqrt(ms + EPS) * g
    return y * (1.0 + scale) + shift


def _swiglu(u_bf16, w_in_ref, w_out_ref):
    hcat = jnp.dot(u_bf16, w_in_ref[...], preferred_element_type=F32)
    a = hcat[:, :D_FF]
    b = hcat[:, D_FF:]
    act = (a * jax.nn.sigmoid(a) * b).astype(BF16)
    return jnp.dot(act, w_out_ref[...], preferred_element_type=F32)


def _ada_kernel(c_ref, w_ref, b_ref, o_ref):
    c = c_ref[...]
    cond = c * jax.nn.sigmoid(c)
    c_hi = cond.astype(BF16)
    c_lo = (cond - c_hi.astype(F32)).astype(BF16)
    w = w_ref[...]
    w_hi = w.astype(BF16)
    w_lo = (w - w_hi.astype(F32)).astype(BF16)
    acc = jnp.dot(c_hi, w_hi, preferred_element_type=F32)
    acc += jnp.dot(c_hi, w_lo, preferred_element_type=F32)
    acc += jnp.dot(c_lo, w_hi, preferred_element_type=F32)
    o_ref[...] = acc + b_ref[...]


def _ada_modulation(c, w_ada, b_ada):
    batch = c.shape[0]
    return pl.pallas_call(
        _ada_kernel,
        grid=(N_MOD,),
        in_specs=[
            pl.BlockSpec((batch, D_MODEL), lambda j: (0, 0)),
            pl.BlockSpec((D_MODEL, D_MODEL), lambda j: (0, j)),
            pl.BlockSpec((1, D_MODEL), lambda j: (0, j)),
        ],
        out_specs=pl.BlockSpec((batch, D_MODEL), lambda j: (0, j)),
        out_shape=jax.ShapeDtypeStruct((batch, N_MOD * D_MODEL), F32),
        compiler_params=pltpu.CompilerParams(dimension_semantics=("parallel",)),
        name="ada_modulation",
    )(c, w_ada, b_ada.reshape(1, N_MOD * D_MODEL))


def _ffn1_proj_kernel(x_ref, mod_ref, g1_ref, g2_ref, pos_ref, invf_ref,
                      w1i_ref, w1o_ref, wp_ref,
                      h_ref, p_ref, kvq0_ref, kvq1_ref, kvq2_ref, stage_ref):
    tm = x_ref.shape[1]
    x = x_ref[0]
    mod = mod_ref[0]
    u = _rms_modulate(x, g1_ref[...], mod[0:1], mod[1:2]).astype(BF16)
    h = x + (0.5 * mod[2:3]) * _swiglu(u, w1i_ref, w1o_ref)
    h_ref[0] = h

    u2 = _rms_modulate(h, g2_ref[...], mod[3:4], mod[4:5]).astype(BF16)
    proj = jnp.dot(u2, wp_ref[...], preferred_element_type=F32)
    p_ref[0] = proj[:, :POOL_WIDTH]

    qbase, kbase, vbase = POOL_WIDTH, POOL_WIDTH + ATTN_WIDTH, POOL_WIDTH + 2 * ATTN_WIDTH
    n_slabs = N_GROUPS * SLABS
    lane = lax.broadcasted_iota(jnp.int32, (QBLK, LANES), 1)
    first_half = (lane & (HEAD_DIM - 1)) < (HEAD_DIM // 2)
    invf = invf_ref[...]
    for c in range(tm // QBLK):
        rows = slice(c * QBLK, (c + 1) * QBLK)
        pos_row = pos_ref[0, 0, c:c + 1, :].astype(F32)
        pos_col = jnp.broadcast_to(pos_row, (QBLK, LANES)).T
        ang = pos_col * invf
        cosf = jnp.cos(ang)
        sinf = jnp.sin(ang)
        sin_signed = jnp.where(first_half, -sinf, sinf)
        for s in range(n_slabs):
            for which, base, scale in ((0, kbase, None), (2, qbase, HEAD_DIM ** -0.5)):
                t = proj[rows, base + s * LANES:base + (s + 1) * LANES]
                rot = jnp.where(first_half, pltpu.roll(t, LANES - HEAD_DIM // 2, 1),
                                pltpu.roll(t, HEAD_DIM // 2, 1))
                r = t * cosf + rot * sin_signed
                if scale is not None:
                    r = r * scale
                stage_ref[which * n_slabs + s, rows, :] = r
    for s in range(n_slabs):
        stage_ref[n_slabs + s] = proj[:, vbase + s * LANES:vbase + (s + 1) * LANES]

    for g, (d, out_ref) in enumerate(zip(ATTN_DILATIONS, (kvq0_ref, kvq1_ref, kvq2_ref))):
        for which in range(3):
            for sl in range(SLABS):
                s = which * n_slabs + g * SLABS + sl
                cols = slice(which * GROUP_WIDTH + sl * LANES, which * GROUP_WIDTH + (sl + 1) * LANES)
                for r in range(d):
                    val = stage_ref[s] if d == 1 else stage_ref[s, pl.ds(r, tm // d, stride=d), :]
                    out_ref[0, r, :, cols] = val.astype(BF16)


def _ffn1_proj(x, mod, g1, g2, pos4, invf, w1i, w1o, wp):
    batch, seq, _ = x.shape
    tm = TOKEN_TILE
    tile = lambda w: pl.BlockSpec((1, tm, w), lambda b, i: (b, i, 0))
    kvq_specs = [pl.BlockSpec((1, d, tm // d, KVQ_WIDTH), lambda b, i: (b, 0, i, 0)) for d in ATTN_DILATIONS]
    kvq_shapes = [jax.ShapeDtypeStruct((batch, d, seq // d, KVQ_WIDTH), BF16) for d in ATTN_DILATIONS]
    return pl.pallas_call(
        _ffn1_proj_kernel,
        grid=(batch, seq // tm),
        in_specs=[
            tile(D_MODEL),
            pl.BlockSpec((1, N_MOD, D_MODEL), lambda b, i: (b, 0, 0)),
            _resident((1, D_MODEL)),
            _resident((1, D_MODEL)),
            pl.BlockSpec((1, 1, tm // LANES, LANES), lambda b, i: (b, i, 0, 0)),
            _resident((1, LANES)),
            _resident(w1i.shape),
            _resident(w1o.shape),
            _resident(wp.shape),
        ],
        out_specs=[tile(D_MODEL), tile(POOL_WIDTH)] + kvq_specs,
        out_shape=[
            jax.ShapeDtypeStruct((batch, seq, D_MODEL), F32),
            jax.ShapeDtypeStruct((batch, seq, POOL_WIDTH), F32),
        ] + kvq_shapes,
        scratch_shapes=[pltpu.VMEM((3 * N_GROUPS * SLABS, tm, LANES), F32)],
        compiler_params=pltpu.CompilerParams(
            dimension_semantics=("parallel", "parallel"),
            vmem_limit_bytes=VMEM_LIMIT_BYTES),
        name="ffn1_proj",
    )(x, mod, g1, g2, pos4, invf, w1i, w1o, wp)


def _attn_kernel(cur_ref, prev_ref, o_ref, lse_ref):
    nq = cur_ref.shape[2] // QBLK
    has_prev = pl.program_id(2) > 0

    a = lax.broadcasted_iota(jnp.int32, (2 * QBLK, 2 * QBLK), 0) & (QBLK - 1)
    c = lax.broadcasted_iota(jnp.int32, (2 * QBLK, 2 * QBLK), 1)
    bias = jnp.where((c >= a) & (c <= a + QBLK), 0.0, NEG_BIG).astype(F32)
    bias_first = jnp.where((c >= QBLK) | has_prev, bias, NEG_BIG)
    head0 = lax.broadcasted_iota(jnp.int32, (QBLK, LANES), 1) < HEAD_DIM
    ones = jnp.ones((2 * QBLK, LANES), BF16)
    dn = (((1,), (1,)), ((), ()))

    def band(j, cols):
        if j == 0:
            return jnp.concatenate([prev_ref[0, 0, :, cols], cur_ref[0, 0, :QBLK, cols]], axis=0)
        return cur_ref[0, 0, (j - 1) * QBLK:(j + 1) * QBLK, cols]

    blocks = [(sl, j) for sl in range(SLABS) for j in range(nq)]
    scores = []
    for sl, j in blocks:
        qs = cur_ref[0, 0, j * QBLK:(j + 1) * QBLK, 2 * GROUP_WIDTH + sl * LANES:2 * GROUP_WIDTH + (sl + 1) * LANES]
        zero = jnp.zeros_like(qs)
        lhs = jnp.concatenate([jnp.where(head0, qs, zero), jnp.where(head0, zero, qs)], axis=0)
        s = lax.dot_general(lhs, band(j, slice(sl * LANES, (sl + 1) * LANES)), dn, preferred_element_type=F32)
        scores.append(s + (bias_first if j == 0 else bias))
    probs = []
    for s in scores:
        m = jnp.max(s, axis=-1, keepdims=True)
        probs.append((jnp.exp(s - m).astype(BF16), m))
    for (sl, j), (p, m) in zip(blocks, probs):
        vband = band(j, slice(GROUP_WIDTH + sl * LANES, GROUP_WIDTH + (sl + 1) * LANES))
        out = jnp.dot(p, jnp.concatenate([vband, ones], axis=1), preferred_element_type=F32)
        num = jnp.where(head0, out[:QBLK, :LANES], out[QBLK:, :LANES])
        den = jnp.where(head0, out[:QBLK, LANES:], out[QBLK:, LANES:])
        mm = jnp.where(head0, jnp.broadcast_to(m[:QBLK], (QBLK, LANES)), jnp.broadcast_to(m[QBLK:], (QBLK, LANES)))
        rows = slice(j * QBLK, (j + 1) * QBLK)
        ocols = slice(sl * LANES, (sl + 1) * LANES)
        o_ref[0, 0, rows, ocols] = num / den
        lse_ref[0, 0, rows, ocols] = mm + jnp.log(den)


def _dilated_attention(kvq):
    batch, d, length, _ = kvq.shape
    n = ATTN_ROWS
    cur = pl.BlockSpec((1, 1, n, KVQ_WIDTH), lambda b, r, i: (b, r, i, 0))
    prev = pl.BlockSpec((1, 1, QBLK, 2 * GROUP_WIDTH),
                        lambda b, r, i: (b, r, jnp.maximum(i * (n // QBLK) - 1, 0), 0))
    out = pl.BlockSpec((1, 1, n, GROUP_WIDTH), lambda b, r, i: (b, r, i, 0))
    out_shape = jax.ShapeDtypeStruct((batch, d, length, GROUP_WIDTH), F32)
    return pl.pallas_call(
        _attn_kernel,
        grid=(batch, d, length // n),
        in_specs=[cur, prev],
        out_specs=[out, out],
        out_shape=[out_shape, out_shape],
        compiler_params=pltpu.CompilerParams(
            dimension_semantics=("parallel", "parallel", "parallel"),
            vmem_limit_bytes=VMEM_LIMIT_BYTES),
        name=f"dilated_attn_d{d}",
    )(kvq, kvq)


def _merge_ffn2_kernel(h_ref, p_ref, pprev_ref, o0_ref, o1_ref, o2_ref, l0_ref, l1_ref, l2_ref,
                       mod_ref, gmix_ref, g3_ref, gfin_ref,
                       wg_ref, wpool_ref, pscale_ref, wpb_ref, wab_ref, wout_ref, w2i_ref, w2o_ref,
                       out_ref, tok_ref):
    tm = h_ref.shape[1]
    i = pl.program_id(1)
    h = h_ref[0]
    mod = mod_ref[0]
    u = _rms_modulate(h, gmix_ref[...], mod[3:4], mod[4:5]).astype(BF16)
    gates = jax.nn.sigmoid(jnp.dot(u, wg_ref[...], preferred_element_type=F32))

    p = p_ref[0]
    halo = jnp.where(i > 0, pprev_ref[0], 0.0)
    xs = jnp.concatenate([halo, p], axis=0)
    sums = {1: xs}
    w = 1
    while w < POOL_WINDOWS[-1]:
        s = sums[w]
        sums[2 * w] = s[w:] + s[:-w]
        w *= 2
    lane = lax.broadcasted_iota(jnp.int32, (tm, POOL_WIDTH), 1)
    tpos = lax.broadcasted_iota(jnp.int32, (tm, POOL_WIDTH), 0) + i * tm
    trail = jnp.zeros((tm, POOL_WIDTH), F32)
    win = jnp.zeros((tm, POOL_WIDTH), jnp.int32)
    for gi, wnd in enumerate(POOL_WINDOWS):
        in_group = (lane >> 6) == gi
        start = POOL_HALO - (wnd - 1)
        trail = jnp.where(in_group, sums[wnd][start:start + tm], trail)
        win = jnp.where(in_group, wnd, win)
    count = jnp.minimum(tpos + 1, win).astype(F32)
    dpool = (trail / count - p).astype(BF16)
    y_pool = jnp.dot(dpool, wpool_ref[...], preferred_element_type=F32) * pscale_ref[...]

    def token_major(ref, d, a):
        if d == 1:
            return ref[0, 0]
        for sl in range(SLABS):
            for r in range(d):
                tok_ref[a * SLABS + sl, pl.ds(r, tm // d, stride=d), :] = ref[0, r, :, sl * LANES:(sl + 1) * LANES]
        return jnp.concatenate([tok_ref[a * SLABS + sl] for sl in range(SLABS)], axis=1)

    o0, l0 = token_major(o0_ref, ATTN_DILATIONS[0], 0), token_major(l0_ref, ATTN_DILATIONS[0], 0)
    o1, l1 = token_major(o1_ref, ATTN_DILATIONS[1], 0), token_major(l1_ref, ATTN_DILATIONS[1], 1)
    o2, l2 = token_major(o2_ref, ATTN_DILATIONS[2], 2), token_major(l2_ref, ATTN_DILATIONS[2], 3)

    mx = jnp.maximum(jnp.maximum(l0, l1), l2)
    e0, e1, e2 = jnp.exp(l0 - mx), jnp.exp(l1 - mx), jnp.exp(l2 - mx)
    y_attn = (e0 * o0 + e1 * o1 + e2 * o2) / (e0 + e1 + e2)

    merged = gates[:, :D_MODEL] * jnp.dot(y_pool.astype(BF16), wpb_ref[...], preferred_element_type=F32)
    merged += gates[:, D_MODEL:] * jnp.dot(y_attn.astype(BF16), wab_ref[...], preferred_element_type=F32)
    h = h + mod[5:6] * jnp.dot(merged.astype(BF16), wout_ref[...], preferred_element_type=F32)

    u3 = _rms_modulate(h, g3_ref[...], mod[6:7], mod[7:8]).astype(BF16)
    h = h + (0.5 * mod[8:9]) * _swiglu(u3, w2i_ref, w2o_ref)

    ms = jnp.mean(h * h, axis=-1, keepdims=True)
    out_ref[0] = h * lax.rsqrt(ms + EPS) * gfin_ref[...]


def _merge_ffn2(h1, p, o, lse, mod, gmix, g3, gfin, wg, wpool, pscale, wpb, wab, wout, w2i, w2o):
    batch, seq, _ = h1.shape
    tm = TOKEN_TILE
    tile = lambda w: pl.BlockSpec((1, tm, w), lambda b, i: (b, i, 0))
    halo = pl.BlockSpec((1, POOL_HALO, POOL_WIDTH),
                        lambda b, i: (b, jnp.maximum(i * (tm // POOL_HALO) - 1, 0), 0))
    grp = [pl.BlockSpec((1, d, tm // d, GROUP_WIDTH), lambda b, i: (b, 0, i, 0)) for d in ATTN_DILATIONS]
    vec = _resident((1, D_MODEL))
    return pl.pallas_call(
        _merge_ffn2_kernel,
        grid=(batch, seq // tm),
        in_specs=[
            tile(D_MODEL), tile(POOL_WIDTH), halo,
            *grp, *grp,
            pl.BlockSpec((1, N_MOD, D_MODEL), lambda b, i: (b, 0, 0)),
            vec, vec, vec,
            _resident(wg.shape), _resident(wpool.shape), _resident(pscale.shape),
            _resident(wpb.shape), _resident(wab.shape), _resident(wout.shape),
            _resident(w2i.shape), _resident(w2o.shape),
        ],
        out_specs=tile(D_MODEL),
        out_shape=jax.ShapeDtypeStruct((batch, seq, D_MODEL), F32),
        scratch_shapes=[pltpu.VMEM((4 * SLABS, tm, LANES), F32)],
        compiler_params=pltpu.CompilerParams(
            dimension_semantics=("parallel", "parallel"),
            vmem_limit_bytes=VMEM_LIMIT_BYTES),
        name="merge_ffn2",
    )(h1, p, p, *o, *lse, mod, gmix, g3, gfin, wg, wpool, pscale, wpb, wab, wout, w2i, w2o)


def kernel(x, c, positions, w_ada, b_ada, g_norm_ffn1, w_ffn1_in, w_ffn1_out, g_norm_mix, w_in, w_pool,
           pool_scale, w_pool_branch, w_attn_branch, w_out, g_norm_ffn2, w_ffn2_in, w_ffn2_out, g_final):
    assert w_ada.shape[0] == 1, "single-layer block"
    batch, seq, d_model = x.shape
    assert d_model == D_MODEL and seq % (ATTN_ROWS * ATTN_DILATIONS[-1]) == 0 and seq % TOKEN_TILE == 0

    mod = _ada_modulation(c, w_ada[0], b_ada[0]).reshape(batch, N_MOD, D_MODEL)

    row = lambda g: g.reshape(1, -1)
    bf = lambda w: w.astype(BF16)
    freq = ROPE_THETA ** (-jnp.arange(0, HEAD_DIM, 2, dtype=F32) / HEAD_DIM)
    invf = jnp.tile(freq, LANES // (HEAD_DIM // 2)).reshape(1, LANES)
    pos4 = positions.reshape(batch, seq // TOKEN_TILE, TOKEN_TILE // LANES, LANES)

    w_in0 = w_in[0]
    h1, p, *kvq = _ffn1_proj(x, mod, row(g_norm_ffn1[0]), row(g_norm_mix[0]), pos4, invf,
                             bf(w_ffn1_in[0]), bf(w_ffn1_out[0]), bf(w_in0[:, :PQKV_WIDTH]))

    o, lse = zip(*[_dilated_attention(t) for t in kvq])

    w_pool_bd = jnp.zeros((POOL_WIDTH, POOL_WIDTH), F32)
    for gi in range(len(POOL_WINDOWS)):
        sl = slice(gi * POOL_GROUP_DIM, (gi + 1) * POOL_GROUP_DIM)
        w_pool_bd = w_pool_bd.at[sl, sl].set(w_pool[0, gi])

    return _merge_ffn2(h1, p, o, lse, mod, row(g_norm_mix[0]), row(g_norm_ffn2[0]), row(g_final),
                       bf(w_in0[:, PQKV_WIDTH:]), bf(w_pool_bd), row(pool_scale[0]),
                       bf(w_pool_branch[0]), bf(w_attn_branch[0]), bf(w_out[0]),
                       bf(w_ffn2_in[0]), bf(w_ffn2_out[0]))
```

```python
import jax
import jax.numpy as jnp
from jax import lax
from jax.experimental import pallas as pl
from jax.experimental.pallas import tpu as pltpu

F32 = jnp.float32
BF16 = jnp.bfloat16

D_MODEL = 1024
POOL_WINDOWS = (2, 4, 8, 16)
POOL_GROUP_DIM = 64
POOL_WIDTH = 256
ATTN_DILATIONS = (1, 4, 16)
N_GROUPS = 3
HEAD_DIM = 64
GROUP_WIDTH = 256
ATTN_WIDTH = N_GROUPS * GROUP_WIDTH
PQKV_WIDTH = POOL_WIDTH + 3 * ATTN_WIDTH
KVQ_WIDTH = 3 * GROUP_WIDTH
D_FF = 2816
N_MOD = 9
ROPE_THETA = 10000.0
EPS = 1e-6
NEG_BIG = -1e30

LANES = 128
VMEM_LIMIT_BYTES = 60 * 1024 * 1024

TOKEN_TILE = 512
FF_CHUNKS = ((0, 1024), (1024, 2048), (2048, D_FF))
ATTN_ROWS = 1024
QBLK = 128
POOL_HALO = 16
SLABS = GROUP_WIDTH // LANES


def _resident(shape):
    nd = len(shape)
    return pl.BlockSpec(shape, lambda *_: (0,) * nd, pipeline_mode=pl.Buffered(1))


def _rms_modulate(x, g, shift, scale):
    ms = jnp.mean(x * x, axis=-1, keepdims=True)
    y = x * lax.rsqrt(ms + EPS) * g
    return y * (1.0 + scale) + shift


def _dot(a, b):
    return jnp.dot(a, b, preferred_element_type=F32)


def _swiglu(u_bf16, w_in_ref, w_out_ref):
    def up(c0, c1):
        return _dot(u_bf16, w_in_ref[:, c0:c1]), _dot(u_bf16, w_in_ref[:, D_FF + c0:D_FF + c1])

    y = None
    pending = up(*FF_CHUNKS[0])
    for idx, (c0, c1) in enumerate(FF_CHUNKS):
        following = up(*FF_CHUNKS[idx + 1]) if idx + 1 < len(FF_CHUNKS) else None
        a, b = pending
        act = (a * jax.nn.sigmoid(a) * b).astype(BF16)
        part = _dot(act, w_out_ref[c0:c1, :])
        y = part if y is None else y + part
        pending = following
    return y


def _ada_kernel(c_ref, w_ref, b_ref, o_ref):
    c = c_ref[...]
    cond = c * jax.nn.sigmoid(c)
    c_hi = cond.astype(BF16)
    c_lo = (cond - c_hi.astype(F32)).astype(BF16)
    w = w_ref[...]
    w_hi = w.astype(BF16)
    w_lo = (w - w_hi.astype(F32)).astype(BF16)
    acc = jnp.dot(c_hi, w_hi, preferred_element_type=F32)
    acc += jnp.dot(c_hi, w_lo, preferred_element_type=F32)
    acc += jnp.dot(c_lo, w_hi, preferred_element_type=F32)
    o_ref[...] = acc + b_ref[...]


def _ada_modulation(c, w_ada, b_ada):
    batch = c.shape[0]
    return pl.pallas_call(
        _ada_kernel,
        grid=(N_MOD,),
        in_specs=[
            pl.BlockSpec((batch, D_MODEL), lambda j: (0, 0)),
            pl.BlockSpec((D_MODEL, D_MODEL), lambda j: (0, j)),
            pl.BlockSpec((1, D_MODEL), lambda j: (0, j)),
        ],
        out_specs=pl.BlockSpec((batch, D_MODEL), lambda j: (0, j)),
        out_shape=jax.ShapeDtypeStruct((batch, N_MOD * D_MODEL), F32),
        compiler_params=pltpu.CompilerParams(dimension_semantics=("parallel",)),
        name="ada_modulation",
    )(c, w_ada, b_ada.reshape(1, N_MOD * D_MODEL))


def _ffn1_proj_kernel(x_ref, mod_ref, g1_ref, g2_ref, pos_ref, invf_ref,
                      w1i_ref, w1o_ref, wp_ref,
                      h_ref, p_ref, kvq0_ref, kvq1_ref, kvq2_ref, trig_ref, stage_ref):
    tm = x_ref.shape[1]
    n_chunks = tm // QBLK
    half = HEAD_DIM // 2
    assert n_chunks * half == LANES

    lane = lax.broadcasted_iota(jnp.int32, (QBLK, LANES), 1)
    quarter = lane >> 5
    first_half = (lane & (HEAD_DIM - 1)) < half
    pos_c = None
    for j in range(n_chunks):
        pos_row = pos_ref[0, 0, j:j + 1, :].astype(F32)
        pos_col = jnp.broadcast_to(pos_row, (QBLK, LANES)).T
        pos_c = pos_col if j == 0 else jnp.where(quarter == j, pos_col, pos_c)
    ang = pos_c * invf_ref[...]
    for t, tbl in enumerate((jnp.cos(ang), jnp.sin(ang))):
        rolled = [tbl] + [pltpu.roll(tbl, half * k, 1) for k in range(1, n_chunks)]
        for j in range(n_chunks):
            full = rolled[(-j) % n_chunks]
            for ql in range(1, n_chunks):
                full = jnp.where(quarter == ql, rolled[(ql - j) % n_chunks], full)
            if t == 1:
                full = jnp.where(first_half, -full, full)
            trig_ref[t, j * QBLK:(j + 1) * QBLK, :] = full

    x = x_ref[0]
    mod = mod_ref[0]
    u = _rms_modulate(x, g1_ref[...], mod[0:1], mod[1:2]).astype(BF16)
    h = x + (0.5 * mod[2:3]) * _swiglu(u, w1i_ref, w1o_ref)
    h_ref[0] = h
    u2 = _rms_modulate(h, g2_ref[...], mod[3:4], mod[4:5]).astype(BF16)

    qbase, kbase, vbase = POOL_WIDTH, POOL_WIDTH + ATTN_WIDTH, POOL_WIDTH + 2 * ATTN_WIDTH
    staged = (N_GROUPS - 1) * SLABS

    def emit(val, which, g, sl):
        if g == 0:
            kvq0_ref[0, 0, :, which * GROUP_WIDTH + sl * LANES:which * GROUP_WIDTH + (sl + 1) * LANES] = val.astype(BF16)
        else:
            stage_ref[which * staged + (g - 1) * SLABS + sl] = val

    def rotary(t):
        rot = jnp.where(jnp.tile(first_half, (n_chunks, 1)), pltpu.roll(t, LANES - half, 1), pltpu.roll(t, half, 1))
        return t * trig_ref[0] + rot * trig_ref[1]

    pk = _dot(u2, wp_ref[:, kbase:kbase + ATTN_WIDTH])
    pq = _dot(u2, wp_ref[:, qbase:qbase + ATTN_WIDTH])
    for s in range(N_GROUPS * SLABS):
        emit(rotary(pk[:, s * LANES:(s + 1) * LANES]), 0, s // SLABS, s % SLABS)
    p_ref[0] = _dot(u2, wp_ref[:, :POOL_WIDTH])
    pv = _dot(u2, wp_ref[:, vbase:vbase + ATTN_WIDTH])
    for s in range(N_GROUPS * SLABS):
        emit(rotary(pq[:, s * LANES:(s + 1) * LANES]), 2, s // SLABS, s % SLABS)
    for s in range(N_GROUPS * SLABS):
        emit(pv[:, s * LANES:(s + 1) * LANES], 1, s // SLABS, s % SLABS)

    for g, out_ref in ((1, kvq1_ref), (2, kvq2_ref)):
        d = ATTN_DILATIONS[g]
        for which in range(3):
            for sl in range(SLABS):
                s = which * staged + (g - 1) * SLABS + sl
                cols = slice(which * GROUP_WIDTH + sl * LANES, which * GROUP_WIDTH + (sl + 1) * LANES)
                for r in range(d):
                    out_ref[0, r, :, cols] = stage_ref[s, pl.ds(r, tm // d, stride=d), :].astype(BF16)


def _ffn1_proj(x, mod, g1, g2, pos4, invf, w1i, w1o, wp):
    batch, seq, _ = x.shape
    tm = TOKEN_TILE
    tile = lambda w: pl.BlockSpec((1, tm, w), lambda b, i: (b, i, 0))
    kvq_specs = [pl.BlockSpec((1, d, tm // d, KVQ_WIDTH), lambda b, i: (b, 0, i, 0)) for d in ATTN_DILATIONS]
    kvq_shapes = [jax.ShapeDtypeStruct((batch, d, seq // d, KVQ_WIDTH), BF16) for d in ATTN_DILATIONS]
    return pl.pallas_call(
        _ffn1_proj_kernel,
        grid=(batch, seq // tm),
        in_specs=[
            tile(D_MODEL),
            pl.BlockSpec((1, N_MOD, D_MODEL), lambda b, i: (b, 0, 0)),
            _resident((1, D_MODEL)),
            _resident((1, D_MODEL)),
            pl.BlockSpec((1, 1, tm // LANES, LANES), lambda b, i: (b, i, 0, 0)),
            _resident((1, LANES)),
            _resident(w1i.shape),
            _resident(w1o.shape),
            _resident(wp.shape),
        ],
        out_specs=[tile(D_MODEL), tile(POOL_WIDTH)] + kvq_specs,
        out_shape=[
            jax.ShapeDtypeStruct((batch, seq, D_MODEL), F32),
            jax.ShapeDtypeStruct((batch, seq, POOL_WIDTH), F32),
        ] + kvq_shapes,
        scratch_shapes=[pltpu.VMEM((2, tm, LANES), F32),
                        pltpu.VMEM((3 * (N_GROUPS - 1) * SLABS, tm, LANES), F32)],
        compiler_params=pltpu.CompilerParams(
            dimension_semantics=("parallel", "parallel"),
            vmem_limit_bytes=VMEM_LIMIT_BYTES),
        name="ffn1_proj",
    )(x, mod, g1, g2, pos4, invf, w1i, w1o, wp)


def _attn_kernel(cur_ref, prev_ref, o_ref, lse_ref):
    n_res = cur_ref.shape[1]
    nq = cur_ref.shape[2] // QBLK
    has_prev = pl.program_id(2) > 0

    a = lax.broadcasted_iota(jnp.int32, (2 * QBLK, 2 * QBLK), 0) & (QBLK - 1)
    c = lax.broadcasted_iota(jnp.int32, (2 * QBLK, 2 * QBLK), 1)
    bias = jnp.where((c >= a) & (c <= a + QBLK), 0.0, NEG_BIG).astype(F32)
    bias_first = jnp.where((c >= QBLK) | has_prev, bias, NEG_BIG)
    head0 = lax.broadcasted_iota(jnp.int32, (QBLK, LANES), 1) < HEAD_DIM
    ones = jnp.ones((2 * QBLK, LANES), BF16)
    dn = (((1,), (1,)), ((), ()))

    def band(r, j, cols):
        if j == 0:
            return jnp.concatenate([prev_ref[0, r, :, cols], cur_ref[0, r, :QBLK, cols]], axis=0)
        return cur_ref[0, r, (j - 1) * QBLK:(j + 1) * QBLK, cols]

    blocks = [(r, sl, j) for r in range(n_res) for sl in range(SLABS) for j in range(nq)]
    scores = []
    for r, sl, j in blocks:
        qs = cur_ref[0, r, j * QBLK:(j + 1) * QBLK, 2 * GROUP_WIDTH + sl * LANES:2 * GROUP_WIDTH + (sl + 1) * LANES]
        zero = jnp.zeros_like(qs)
        lhs = jnp.concatenate([jnp.where(head0, qs, zero), jnp.where(head0, zero, qs)], axis=0)
        s = lax.dot_general(lhs, band(r, j, slice(sl * LANES, (sl + 1) * LANES)), dn, preferred_element_type=F32)
        scores.append(s + (bias_first if j == 0 else bias))
    probs = []
    for s in scores:
        m = jnp.max(s, axis=-1, keepdims=True)
        probs.append((jnp.exp(s - m).astype(BF16), m))
    for (r, sl, j), (p, m) in zip(blocks, probs):
        vband = band(r, j, slice(GROUP_WIDTH + sl * LANES, GROUP_WIDTH + (sl + 1) * LANES))
        out = jnp.dot(p, jnp.concatenate([vband, ones], axis=1), preferred_element_type=F32)
        num = jnp.where(head0, out[:QBLK, :LANES], out[QBLK:, :LANES])
        den = jnp.where(head0, out[:QBLK, LANES:], out[QBLK:, LANES:])
        mm = jnp.where(head0, jnp.broadcast_to(m[:QBLK], (QBLK, LANES)), jnp.broadcast_to(m[QBLK:], (QBLK, LANES)))
        rows = slice(j * QBLK, (j + 1) * QBLK)
        ocols = slice(sl * LANES, (sl + 1) * LANES)
        o_ref[0, r, rows, ocols] = num / den
        lse_ref[0, r, rows, ocols] = mm + jnp.log(den)


def _dilated_attention(kvq):
    batch, d, length, _ = kvq.shape
    n = min(ATTN_ROWS, length)
    n_res = ATTN_ROWS // n
    cur = pl.BlockSpec((1, n_res, n, KVQ_WIDTH), lambda b, r, i: (b, r, i, 0))
    prev = pl.BlockSpec((1, n_res, QBLK, 2 * GROUP_WIDTH),
                        lambda b, r, i: (b, r, jnp.maximum(i * (n // QBLK) - 1, 0), 0))
    out = pl.BlockSpec((1, n_res, n, GROUP_WIDTH), lambda b, r, i: (b, r, i, 0))
    out_shape = jax.ShapeDtypeStruct((batch, d, length, GROUP_WIDTH), F32)
    return pl.pallas_call(
        _attn_kernel,
        grid=(batch, d // n_res, length // n),
        in_specs=[cur, prev],
        out_specs=[out, out],
        out_shape=[out_shape, out_shape],
        compiler_params=pltpu.CompilerParams(
            dimension_semantics=("parallel", "parallel", "parallel"),
            vmem_limit_bytes=VMEM_LIMIT_BYTES),
        name=f"dilated_attn_d{d}",
    )(kvq, kvq)


def _merge_ffn2_kernel(h_ref, p_ref, pprev_ref, o0_ref, o1_ref, o2_ref, l0_ref, l1_ref, l2_ref,
                       mod_ref, gmix_ref, g3_ref, gfin_ref,
                       wg_ref, wpool_ref, pscale_ref, wpb_ref, wab_ref, wout_ref, w2i_ref, w2o_ref,
                       out_ref, tok_ref):
    tm = h_ref.shape[1]
    i = pl.program_id(1)
    h = h_ref[0]
    mod = mod_ref[0]
    u = _rms_modulate(h, gmix_ref[...], mod[3:4], mod[4:5]).astype(BF16)
    gates = jax.nn.sigmoid(_dot(u, wg_ref[...]))

    p = p_ref[0]
    halo = jnp.where(i > 0, pprev_ref[0], 0.0)
    xs = jnp.concatenate([halo, p], axis=0)
    sums = {1: xs}
    w = 1
    while w < POOL_WINDOWS[-1]:
        s = sums[w]
        sums[2 * w] = s[w:] + s[:-w]
        w *= 2
    lane = lax.broadcasted_iota(jnp.int32, (tm, POOL_WIDTH), 1)
    tpos = lax.broadcasted_iota(jnp.int32, (tm, POOL_WIDTH), 0) + i * tm
    trail = jnp.zeros((tm, POOL_WIDTH), F32)
    win = jnp.zeros((tm, POOL_WIDTH), jnp.int32)
    for gi, wnd in enumerate(POOL_WINDOWS):
        in_group = (lane >> 6) == gi
        start = POOL_HALO - (wnd - 1)
        trail = jnp.where(in_group, sums[wnd][start:start + tm], trail)
        win = jnp.where(in_group, wnd, win)
    count = jnp.minimum(tpos + 1, win).astype(F32)
    dpool = (trail / count - p).astype(BF16)
    y_pool = _dot(dpool, wpool_ref[...]) * pscale_ref[...]

    def token_major(ref, d, a):
        if d == 1:
            return ref[0, 0]
        for sl in range(SLABS):
            for r in range(d):
                tok_ref[a * SLABS + sl, pl.ds(r, tm // d, stride=d), :] = ref[0, r, :, sl * LANES:(sl + 1) * LANES]
        return jnp.concatenate([tok_ref[a * SLABS + sl] for sl in range(SLABS)], axis=1)

    o0, l0 = token_major(o0_ref, ATTN_DILATIONS[0], 0), token_major(l0_ref, ATTN_DILATIONS[0], 0)
    o1, l1 = token_major(o1_ref, ATTN_DILATIONS[1], 0), token_major(l1_ref, ATTN_DILATIONS[1], 1)
    o2, l2 = token_major(o2_ref, ATTN_DILATIONS[2], 2), token_major(l2_ref, ATTN_DILATIONS[2], 3)

    mx = jnp.maximum(jnp.maximum(l0, l1), l2)
    e0, e1, e2 = jnp.exp(l0 - mx), jnp.exp(l1 - mx), jnp.exp(l2 - mx)
    y_attn = (e0 * o0 + e1 * o1 + e2 * o2) / (e0 + e1 + e2)

    merged = gates[:, :D_MODEL] * _dot(y_pool.astype(BF16), wpb_ref[...])
    merged += gates[:, D_MODEL:] * _dot(y_attn.astype(BF16), wab_ref[...])
    h = h + mod[5:6] * _dot(merged.astype(BF16), wout_ref[...])

    u3 = _rms_modulate(h, g3_ref[...], mod[6:7], mod[7:8]).astype(BF16)
    h = h + (0.5 * mod[8:9]) * _swiglu(u3, w2i_ref, w2o_ref)

    ms = jnp.mean(h * h, axis=-1, keepdims=True)
    out_ref[0] = h * lax.rsqrt(ms + EPS) * gfin_ref[...]


def _merge_ffn2(h1, p, o, lse, mod, gmix, g3, gfin, wg, wpool, pscale, wpb, wab, wout, w2i, w2o):
    batch, seq, _ = h1.shape
    tm = TOKEN_TILE
    tile = lambda w: pl.BlockSpec((1, tm, w), lambda b, i: (b, i, 0))
    halo = pl.BlockSpec((1, POOL_HALO, POOL_WIDTH),
                        lambda b, i: (b, jnp.maximum(i * (tm // POOL_HALO) - 1, 0), 0))
    grp = [pl.BlockSpec((1, d, tm // d, GROUP_WIDTH), lambda b, i: (b, 0, i, 0)) for d in ATTN_DILATIONS]
    vec = _resident((1, D_MODEL))
    return pl.pallas_call(
        _merge_ffn2_kernel,
        grid=(batch, seq // tm),
        in_specs=[
            tile(D_MODEL), tile(POOL_WIDTH), halo,
            *grp, *grp,
            pl.BlockSpec((1, N_MOD, D_MODEL), lambda b, i: (b, 0, 0)),
            vec, vec, vec,
            _resident(wg.shape), _resident(wpool.shape), _resident(pscale.shape),
            _resident(wpb.shape), _resident(wab.shape), _resident(wout.shape),
            _resident(w2i.shape), _resident(w2o.shape),
        ],
        out_specs=tile(D_MODEL),
        out_shape=jax.ShapeDtypeStruct((batch, seq, D_MODEL), F32),
        scratch_shapes=[pltpu.VMEM((4 * SLABS, tm, LANES), F32)],
        compiler_params=pltpu.CompilerParams(
            dimension_semantics=("parallel", "parallel"),
            vmem_limit_bytes=VMEM_LIMIT_BYTES),
        name="merge_ffn2",
    )(h1, p, p, *o, *lse, mod, gmix, g3, gfin, wg, wpool, pscale, wpb, wab, wout, w2i, w2o)


def kernel(x, c, positions, w_ada, b_ada, g_norm_ffn1, w_ffn1_in, w_ffn1_out, g_norm_mix, w_in, w_pool,
           pool_scale, w_pool_branch, w_attn_branch, w_out, g_norm_ffn2, w_ffn2_in, w_ffn2_out, g_final):
    assert w_ada.shape[0] == 1, "single-layer block"
    batch, seq, d_model = x.shape
    assert d_model == D_MODEL and seq % TOKEN_TILE == 0
    assert all((seq // d) % QBLK == 0 and ATTN_ROWS % min(ATTN_ROWS, seq // d) == 0 for d in ATTN_DILATIONS)

    mod = _ada_modulation(c, w_ada[0], b_ada[0]).reshape(batch, N_MOD, D_MODEL)

    row = lambda g: g.reshape(1, -1)
    bf = lambda w: w.astype(BF16)
    freq = ROPE_THETA ** (-jnp.arange(0, HEAD_DIM, 2, dtype=F32) / HEAD_DIM)
    invf = jnp.tile(freq, LANES // (HEAD_DIM // 2)).reshape(1, LANES)
    pos4 = positions.reshape(batch, seq // TOKEN_TILE, TOKEN_TILE // LANES, LANES)

    col = jnp.arange(PQKV_WIDTH)
    q_cols = (col >= POOL_WIDTH) & (col < POOL_WIDTH + ATTN_WIDTH)
    w_in0 = w_in[0]
    w_pqkv = w_in0[:, :PQKV_WIDTH] * jnp.where(q_cols, HEAD_DIM ** -0.5, 1.0).astype(F32)
    h1, p, *kvq = _ffn1_proj(x, mod, row(g_norm_ffn1[0]), row(g_norm_mix[0]), pos4, invf,
                             bf(w_ffn1_in[0]), bf(w_ffn1_out[0]), bf(w_pqkv))

    o, lse = zip(*[_dilated_attention(t) for t in kvq])

    w_pool_bd = jnp.zeros((POOL_WIDTH, POOL_WIDTH), F32)
    for gi in range(len(POOL_WINDOWS)):
        sl = slice(gi * POOL_GROUP_DIM, (gi + 1) * POOL_GROUP_DIM)
        w_pool_bd = w_pool_bd.at[sl, sl].set(w_pool[0, gi])

    return _merge_ffn2(h1, p, o, lse, mod, row(g_norm_mix[0]), row(g_norm_ffn2[0]), row(g_final),
                       bf(w_in0[:, PQKV_WIDTH:]), bf(w_pool_bd), row(pool_scale[0]),
                       bf(w_pool_branch[0]), bf(w_attn_branch[0]), bf(w_out[0]),
                       bf(w_ffn2_in[0]), bf(w_ffn2_out[0]))
```

```python
import jax
import jax.numpy as jnp
from jax import lax
from jax.experimental import pallas as pl
from jax.experimental.pallas import tpu as pltpu

F32 = jnp.float32
BF16 = jnp.bfloat16

D_MODEL = 1024
POOL_WINDOWS = (2, 4, 8, 16)
POOL_GROUP_DIM = 64
POOL_WIDTH = 256
ATTN_DILATIONS = (1, 4, 16)
N_GROUPS = 3
HEAD_DIM = 64
GROUP_WIDTH = 256
ATTN_WIDTH = N_GROUPS * GROUP_WIDTH
PQKV_WIDTH = POOL_WIDTH + 3 * ATTN_WIDTH
KVQ_WIDTH = 3 * GROUP_WIDTH
D_FF = 2816
N_MOD = 9
ROPE_THETA = 10000.0
EPS = 1e-6
NEG_BIG = -1e30

LANES = 128
VMEM_LIMIT_BYTES = 60 * 1024 * 1024

TOKEN_TILE = 512
FF_CHUNKS = ((0, 1024), (1024, 2048), (2048, D_FF))
ATTN_ROWS = 1024
QBLK = 128
POOL_HALO = 16
SLABS = GROUP_WIDTH // LANES
ROW_PARTS = 2
GATE_CHUNKS = 2
FINAL_PARTS = 4


def _resident(shape):
    nd = len(shape)
    return pl.BlockSpec(shape, lambda *_: (0,) * nd, pipeline_mode=pl.Buffered(1))


def _rms_modulate(x, g, shift, scale):
    ms = jnp.mean(x * x, axis=-1, keepdims=True)
    y = x * lax.rsqrt(ms + EPS) * g
    return y * (1.0 + scale) + shift


def _dot(a, b):
    return jnp.dot(a, b, preferred_element_type=F32)


def _sigmoid(x):
    return 0.5 * jnp.tanh(0.5 * x) + 0.5


def _row_parts(x, parts):
    n = x.shape[0] // parts
    return [x[i * n:(i + 1) * n] for i in range(parts)]


def _dot_parts(parts, w):
    return jnp.concatenate([_dot(part, w) for part in parts], axis=0)


def _swiglu(u_parts, w_in_ref, w_out_ref, out_parts):
    u = jnp.concatenate(u_parts, axis=0)

    def up(idx):
        c0, c1 = FF_CHUNKS[idx]
        wa, wb = w_in_ref[:, c0:c1], w_in_ref[:, D_FF + c0:D_FF + c1]
        if idx == 0:
            return _dot_parts(u_parts, wa), _dot_parts(u_parts, wb)
        return _dot(u, wa), _dot(u, wb)

    y = None
    pending = up(0)
    last = len(FF_CHUNKS) - 1
    for idx, (c0, c1) in enumerate(FF_CHUNKS):
        following = up(idx + 1) if idx < last else None
        a, b = pending
        act = (a * _sigmoid(a) * b).astype(BF16)
        if idx == last:
            return [yp + _dot(ap, w_out_ref[c0:c1, :])
                    for yp, ap in zip(_row_parts(y, out_parts), _row_parts(act, out_parts))]
        part = _dot(act, w_out_ref[c0:c1, :])
        y = part if y is None else y + part
        pending = following


def _ada_kernel(c_ref, w_ref, b_ref, o_ref):
    c = c_ref[...]
    cond = c * jax.nn.sigmoid(c)
    c_hi = cond.astype(BF16)
    c_lo = (cond - c_hi.astype(F32)).astype(BF16)
    w = w_ref[...]
    w_hi = w.astype(BF16)
    w_lo = (w - w_hi.astype(F32)).astype(BF16)
    acc = jnp.dot(c_hi, w_hi, preferred_element_type=F32)
    acc += jnp.dot(c_hi, w_lo, preferred_element_type=F32)
    acc += jnp.dot(c_lo, w_hi, preferred_element_type=F32)
    o_ref[...] = acc + b_ref[...]


def _ada_modulation(c, w_ada, b_ada):
    batch = c.shape[0]
    return pl.pallas_call(
        _ada_kernel,
        grid=(N_MOD,),
        in_specs=[
            pl.BlockSpec((batch, D_MODEL), lambda j: (0, 0)),
            pl.BlockSpec((D_MODEL, D_MODEL), lambda j: (0, j)),
            pl.BlockSpec((1, D_MODEL), lambda j: (0, j)),
        ],
        out_specs=pl.BlockSpec((batch, D_MODEL), lambda j: (0, j)),
        out_shape=jax.ShapeDtypeStruct((batch, N_MOD * D_MODEL), F32),
        compiler_params=pltpu.CompilerParams(dimension_semantics=("parallel",)),
        name="ada_modulation",
    )(c, w_ada, b_ada.reshape(1, N_MOD * D_MODEL))


def _ffn1_proj_kernel(x_ref, mod_ref, g1_ref, g2_ref, pos_ref, invf_ref,
                      w1i_ref, w1o_ref, wp_ref,
                      h_ref, p_ref, kvq0_ref, kvq1_ref, kvq2_ref, trig_ref, stage_ref):
    tm = x_ref.shape[1]
    n_chunks = tm // QBLK
    half = HEAD_DIM // 2
    assert n_chunks * half == LANES

    lane = lax.broadcasted_iota(jnp.int32, (QBLK, LANES), 1)
    quarter = lane >> 5
    first_half = (lane & (HEAD_DIM - 1)) < half
    pos_c = None
    for j in range(n_chunks):
        pos_row = pos_ref[0, 0, j:j + 1, :].astype(F32)
        pos_col = jnp.broadcast_to(pos_row, (QBLK, LANES)).T
        pos_c = pos_col if j == 0 else jnp.where(quarter == j, pos_col, pos_c)
    ang = pos_c * invf_ref[...]
    for t, tbl in enumerate((jnp.cos(ang), jnp.sin(ang))):
        rolled = [tbl] + [pltpu.roll(tbl, half * k, 1) for k in range(1, n_chunks)]
        for j in range(n_chunks):
            full = rolled[(-j) % n_chunks]
            for ql in range(1, n_chunks):
                full = jnp.where(quarter == ql, rolled[(ql - j) % n_chunks], full)
            if t == 1:
                full = jnp.where(first_half, -full, full)
            trig_ref[t, j * QBLK:(j + 1) * QBLK, :] = full

    x = x_ref[0]
    mod = mod_ref[0]
    u = _rms_modulate(x, g1_ref[...], mod[0:1], mod[1:2]).astype(BF16)
    h = x + (0.5 * mod[2:3]) * _swiglu([u], w1i_ref, w1o_ref, 1)[0]
    h_ref[0] = h
    u2 = _rms_modulate(h, g2_ref[...], mod[3:4], mod[4:5]).astype(BF16)

    qbase, kbase, vbase = POOL_WIDTH, POOL_WIDTH + ATTN_WIDTH, POOL_WIDTH + 2 * ATTN_WIDTH
    staged = (N_GROUPS - 1) * SLABS

    def emit(val, which, g, sl):
        if g == 0:
            kvq0_ref[0, 0, :, which * GROUP_WIDTH + sl * LANES:which * GROUP_WIDTH + (sl + 1) * LANES] = val.astype(BF16)
        else:
            stage_ref[which * staged + (g - 1) * SLABS + sl] = val

    def rotary(t):
        rot = jnp.where(jnp.tile(first_half, (n_chunks, 1)), pltpu.roll(t, LANES - half, 1), pltpu.roll(t, half, 1))
        return t * trig_ref[0] + rot * trig_ref[1]

    pk = _dot(u2, wp_ref[:, kbase:kbase + ATTN_WIDTH])
    pq = _dot(u2, wp_ref[:, qbase:qbase + ATTN_WIDTH])
    for s in range(N_GROUPS * SLABS):
        emit(rotary(pk[:, s * LANES:(s + 1) * LANES]), 0, s // SLABS, s % SLABS)
    p_ref[0] = _dot(u2, wp_ref[:, :POOL_WIDTH])
    pv = _dot(u2, wp_ref[:, vbase:vbase + ATTN_WIDTH])
    for s in range(N_GROUPS * SLABS):
        emit(rotary(pq[:, s * LANES:(s + 1) * LANES]), 2, s // SLABS, s % SLABS)
    for s in range(N_GROUPS * SLABS):
        emit(pv[:, s * LANES:(s + 1) * LANES], 1, s // SLABS, s % SLABS)

    for g, out_ref in ((1, kvq1_ref), (2, kvq2_ref)):
        d = ATTN_DILATIONS[g]
        for which in range(3):
            for sl in range(SLABS):
                s = which * staged + (g - 1) * SLABS + sl
                cols = slice(which * GROUP_WIDTH + sl * LANES, which * GROUP_WIDTH + (sl + 1) * LANES)
                for r in range(d):
                    out_ref[0, r, :, cols] = stage_ref[s, pl.ds(r, tm // d, stride=d), :].astype(BF16)


def _ffn1_proj(x, mod, g1, g2, pos4, invf, w1i, w1o, wp):
    batch, seq, _ = x.shape
    tm = TOKEN_TILE
    tile = lambda w: pl.BlockSpec((1, tm, w), lambda b, i: (b, i, 0))
    kvq_specs = [pl.BlockSpec((1, d, tm // d, KVQ_WIDTH), lambda b, i: (b, 0, i, 0)) for d in ATTN_DILATIONS]
    kvq_shapes = [jax.ShapeDtypeStruct((batch, d, seq // d, KVQ_WIDTH), BF16) for d in ATTN_DILATIONS]
    return pl.pallas_call(
        _ffn1_proj_kernel,
        grid=(batch, seq // tm),
        in_specs=[
            tile(D_MODEL),
            pl.BlockSpec((1, N_MOD, D_MODEL), lambda b, i: (b, 0, 0)),
            _resident((1, D_MODEL)),
            _resident((1, D_MODEL)),
            pl.BlockSpec((1, 1, tm // LANES, LANES), lambda b, i: (b, i, 0, 0)),
            _resident((1, LANES)),
            _resident(w1i.shape),
            _resident(w1o.shape),
            _resident(wp.shape),
        ],
        out_specs=[tile(D_MODEL), tile(POOL_WIDTH)] + kvq_specs,
        out_shape=[
            jax.ShapeDtypeStruct((batch, seq, D_MODEL), F32),
            jax.ShapeDtypeStruct((batch, seq, POOL_WIDTH), F32),
        ] + kvq_shapes,
        scratch_shapes=[pltpu.VMEM((2, tm, LANES), F32),
                        pltpu.VMEM((3 * (N_GROUPS - 1) * SLABS, tm, LANES), F32)],
        compiler_params=pltpu.CompilerParams(
            dimension_semantics=("parallel", "parallel"),
            vmem_limit_bytes=VMEM_LIMIT_BYTES),
        name="ffn1_proj",
    )(x, mod, g1, g2, pos4, invf, w1i, w1o, wp)


def _attn_kernel(cur_ref, prev_ref, o_ref, lse_ref):
    n_res = cur_ref.shape[1]
    nq = cur_ref.shape[2] // QBLK
    has_prev = pl.program_id(2) > 0

    a = lax.broadcasted_iota(jnp.int32, (2 * QBLK, 2 * QBLK), 0) & (QBLK - 1)
    c = lax.broadcasted_iota(jnp.int32, (2 * QBLK, 2 * QBLK), 1)
    bias = jnp.where((c >= a) & (c <= a + QBLK), 0.0, NEG_BIG).astype(F32)
    bias_first = jnp.where((c >= QBLK) | has_prev, bias, NEG_BIG)
    head0 = lax.broadcasted_iota(jnp.int32, (QBLK, LANES), 1) < HEAD_DIM
    ones = jnp.ones((2 * QBLK, LANES), BF16)
    dn = (((1,), (1,)), ((), ()))

    def band(r, j, cols):
        if j == 0:
            return jnp.concatenate([prev_ref[0, r, :, cols], cur_ref[0, r, :QBLK, cols]], axis=0)
        return cur_ref[0, r, (j - 1) * QBLK:(j + 1) * QBLK, cols]

    blocks = [(r, sl, j) for r in range(n_res) for sl in range(SLABS) for j in range(nq)]
    scores = []
    for r, sl, j in blocks:
        qs = cur_ref[0, r, j * QBLK:(j + 1) * QBLK, 2 * GROUP_WIDTH + sl * LANES:2 * GROUP_WIDTH + (sl + 1) * LANES]
        zero = jnp.zeros_like(qs)
        lhs = jnp.concatenate([jnp.where(head0, qs, zero), jnp.where(head0, zero, qs)], axis=0)
        s = lax.dot_general(lhs, band(r, j, slice(sl * LANES, (sl + 1) * LANES)), dn, preferred_element_type=F32)
        scores.append(s + (bias_first if j == 0 else bias))
    probs = []
    for s in scores:
        m = jnp.max(s, axis=-1, keepdims=True)
        probs.append((jnp.exp(s - m).astype(BF16), m))
    for (r, sl, j), (p, m) in zip(blocks, probs):
        vband = band(r, j, slice(GROUP_WIDTH + sl * LANES, GROUP_WIDTH + (sl + 1) * LANES))
        out = jnp.dot(p, jnp.concatenate([vband, ones], axis=1), preferred_element_type=F32)
        num = jnp.where(head0, out[:QBLK, :LANES], out[QBLK:, :LANES])
        den = jnp.where(head0, out[:QBLK, LANES:], out[QBLK:, LANES:])
        mm = jnp.where(head0, jnp.broadcast_to(m[:QBLK], (QBLK, LANES)), jnp.broadcast_to(m[QBLK:], (QBLK, LANES)))
        rows = slice(j * QBLK, (j + 1) * QBLK)
        ocols = slice(sl * LANES, (sl + 1) * LANES)
        o_ref[0, r, rows, ocols] = num / den
        lse_ref[0, r, rows, ocols] = mm + jnp.log(den)


def _dilated_attention(kvq):
    batch, d, length, _ = kvq.shape
    n = min(ATTN_ROWS, length)
    n_res = ATTN_ROWS // n
    cur = pl.BlockSpec((1, n_res, n, KVQ_WIDTH), lambda b, r, i: (b, r, i, 0))
    prev = pl.BlockSpec((1, n_res, QBLK, 2 * GROUP_WIDTH),
                        lambda b, r, i: (b, r, jnp.maximum(i * (n // QBLK) - 1, 0), 0))
    out = pl.BlockSpec((1, n_res, n, GROUP_WIDTH), lambda b, r, i: (b, r, i, 0))
    out_shape = jax.ShapeDtypeStruct((batch, d, length, GROUP_WIDTH), F32)
    return pl.pallas_call(
        _attn_kernel,
        grid=(batch, d // n_res, length // n),
        in_specs=[cur, prev],
        out_specs=[out, out],
        out_shape=[out_shape, out_shape],
        compiler_params=pltpu.CompilerParams(
            dimension_semantics=("parallel", "parallel", "parallel"),
            vmem_limit_bytes=VMEM_LIMIT_BYTES),
        name=f"dilated_attn_d{d}",
    )(kvq, kvq)


def _merge_ffn2_kernel(h_ref, p_ref, pprev_ref, o0_ref, o1_ref, o2_ref, l0_ref, l1_ref, l2_ref,
                       mod_ref, gmix_ref, g3_ref, gfin_ref,
                       wg_ref, wpool_ref, pscale_ref, wpb_ref, wab_ref, wout_ref, w2i_ref, w2o_ref,
                       out_ref, tok_ref):
    tm = h_ref.shape[1]
    i = pl.program_id(1)
    mod = mod_ref[0]
    h_parts = _row_parts(h_ref[0], ROW_PARTS)
    u_parts = [_rms_modulate(hp, gmix_ref[...], mod[3:4], mod[4:5]).astype(BF16) for hp in h_parts]
    u = jnp.concatenate(u_parts, axis=0)
    z_pool = _dot_parts(u_parts, wg_ref[:, :D_MODEL])

    p = p_ref[0]
    halo = jnp.where(i > 0, pprev_ref[0], 0.0)
    xs = jnp.concatenate([halo, p], axis=0)
    sums = {1: xs}
    w = 1
    while w < POOL_WINDOWS[-1]:
        s = sums[w]
        sums[2 * w] = s[w:] + s[:-w]
        w *= 2
    lane = lax.broadcasted_iota(jnp.int32, (tm, POOL_WIDTH), 1)
    tpos = lax.broadcasted_iota(jnp.int32, (tm, POOL_WIDTH), 0) + i * tm
    trail = jnp.zeros((tm, POOL_WIDTH), F32)
    win = jnp.zeros((tm, POOL_WIDTH), jnp.int32)
    for gi, wnd in enumerate(POOL_WINDOWS):
        in_group = (lane >> 6) == gi
        start = POOL_HALO - (wnd - 1)
        trail = jnp.where(in_group, sums[wnd][start:start + tm], trail)
        win = jnp.where(in_group, wnd, win)
    count = jnp.minimum(tpos + 1, win).astype(F32)
    dpool = (trail / count - p).astype(BF16)
    y_pool = (_dot(dpool, wpool_ref[...]) * pscale_ref[...]).astype(BF16)
    b_pool = _dot(y_pool, wpb_ref[...])

    gw = D_MODEL // GATE_CHUNKS
    z_attn = [_dot(u, wg_ref[:, D_MODEL + c * gw:D_MODEL + (c + 1) * gw]) for c in range(GATE_CHUNKS)]

    def token_major(ref, d, a):
        if d == 1:
            return ref[0, 0]
        for sl in range(SLABS):
            for r in range(d):
                tok_ref[a * SLABS + sl, pl.ds(r, tm // d, stride=d), :] = ref[0, r, :, sl * LANES:(sl + 1) * LANES]
        return jnp.concatenate([tok_ref[a * SLABS + sl] for sl in range(SLABS)], axis=1)

    o0, l0 = token_major(o0_ref, ATTN_DILATIONS[0], 0), token_major(l0_ref, ATTN_DILATIONS[0], 0)
    o1, l1 = token_major(o1_ref, ATTN_DILATIONS[1], 0), token_major(l1_ref, ATTN_DILATIONS[1], 1)
    o2, l2 = token_major(o2_ref, ATTN_DILATIONS[2], 2), token_major(l2_ref, ATTN_DILATIONS[2], 3)

    mx = jnp.maximum(jnp.maximum(l0, l1), l2)
    e0, e1, e2 = jnp.exp(l0 - mx), jnp.exp(l1 - mx), jnp.exp(l2 - mx)
    y_attn = ((e0 * o0 + e1 * o1 + e2 * o2) / (e0 + e1 + e2)).astype(BF16)
    b_attn = _dot(y_attn, wab_ref[...])

    gated_pool = _sigmoid(z_pool) * b_pool
    mix = None
    for c in range(GATE_CHUNKS):
        cols = slice(c * gw, (c + 1) * gw)
        merged = (gated_pool[:, cols] + _sigmoid(z_attn[c]) * b_attn[:, cols]).astype(BF16)
        w_rows = wout_ref[cols, :]
        if c + 1 < GATE_CHUNKS:
            part = _dot(merged, w_rows)
            mix = part if mix is None else mix + part
        else:
            mix_parts = [mp + _dot(gp, w_rows)
                         for mp, gp in zip(_row_parts(mix, ROW_PARTS), _row_parts(merged, ROW_PARTS))]
    h_parts = [hp + mod[5:6] * mp for hp, mp in zip(h_parts, mix_parts)]

    u3_parts = [_rms_modulate(hp, g3_ref[...], mod[6:7], mod[7:8]).astype(BF16) for hp in h_parts]
    y_parts = _swiglu(u3_parts, w2i_ref, w2o_ref, FINAL_PARTS)
    n = tm // FINAL_PARTS
    hf_parts = _row_parts(jnp.concatenate(h_parts, axis=0), FINAL_PARTS)
    for part, (hp, yp) in enumerate(zip(hf_parts, y_parts)):
        hf = hp + (0.5 * mod[8:9]) * yp
        ms = jnp.mean(hf * hf, axis=-1, keepdims=True)
        out_ref[0, part * n:(part + 1) * n, :] = hf * lax.rsqrt(ms + EPS) * gfin_ref[...]


def _merge_ffn2(h1, p, o, lse, mod, gmix, g3, gfin, wg, wpool, pscale, wpb, wab, wout, w2i, w2o):
    batch, seq, _ = h1.shape
    tm = TOKEN_TILE
    tile = lambda w: pl.BlockSpec((1, tm, w), lambda b, i: (b, i, 0))
    halo = pl.BlockSpec((1, POOL_HALO, POOL_WIDTH),
                        lambda b, i: (b, jnp.maximum(i * (tm // POOL_HALO) - 1, 0), 0))
    grp = [pl.BlockSpec((1, d, tm // d, GROUP_WIDTH), lambda b, i: (b, 0, i, 0)) for d in ATTN_DILATIONS]
    vec = _resident((1, D_MODEL))
    return pl.pallas_call(
        _merge_ffn2_kernel,
        grid=(batch, seq // tm),
        in_specs=[
            tile(D_MODEL), tile(POOL_WIDTH), halo,
            *grp, *grp,
            pl.BlockSpec((1, N_MOD, D_MODEL), lambda b, i: (b, 0, 0)),
            vec, vec, vec,
            _resident(wg.shape), _resident(wpool.shape), _resident(pscale.shape),
            _resident(wpb.shape), _resident(wab.shape), _resident(wout.shape),
            _resident(w2i.shape), _resident(w2o.shape),
        ],
        out_specs=tile(D_MODEL),
        out_shape=jax.ShapeDtypeStruct((batch, seq, D_MODEL), F32),
        scratch_shapes=[pltpu.VMEM((4 * SLABS, tm, LANES), F32)],
        compiler_params=pltpu.CompilerParams(
            dimension_semantics=("parallel", "parallel"),
            vmem_limit_bytes=VMEM_LIMIT_BYTES),
        name="merge_ffn2",
    )(h1, p, p, *o, *lse, mod, gmix, g3, gfin, wg, wpool, pscale, wpb, wab, wout, w2i, w2o)


def kernel(x, c, positions, w_ada, b_ada, g_norm_ffn1, w_ffn1_in, w_ffn1_out, g_norm_mix, w_in, w_pool,
           pool_scale, w_pool_branch, w_attn_branch, w_out, g_norm_ffn2, w_ffn2_in, w_ffn2_out, g_final):
    assert w_ada.shape[0] == 1, "single-layer block"
    batch, seq, d_model = x.shape
    assert d_model == D_MODEL and seq % TOKEN_TILE == 0
    assert all((seq // d) % QBLK == 0 and ATTN_ROWS % min(ATTN_ROWS, seq // d) == 0 for d in ATTN_DILATIONS)

    mod = _ada_modulation(c, w_ada[0], b_ada[0]).reshape(batch, N_MOD, D_MODEL)

    row = lambda g: g.reshape(1, -1)
    bf = lambda w: w.astype(BF16)
    freq = ROPE_THETA ** (-jnp.arange(0, HEAD_DIM, 2, dtype=F32) / HEAD_DIM)
    invf = jnp.tile(freq, LANES // (HEAD_DIM // 2)).reshape(1, LANES)
    pos4 = positions.reshape(batch, seq // TOKEN_TILE, TOKEN_TILE // LANES, LANES)

    col = jnp.arange(PQKV_WIDTH)
    q_cols = (col >= POOL_WIDTH) & (col < POOL_WIDTH + ATTN_WIDTH)
    w_in0 = w_in[0]
    w_pqkv = w_in0[:, :PQKV_WIDTH] * jnp.where(q_cols, HEAD_DIM ** -0.5, 1.0).astype(F32)
    h1, p, *kvq = _ffn1_proj(x, mod, row(g_norm_ffn1[0]), row(g_norm_mix[0]), pos4, invf,
                             bf(w_ffn1_in[0]), bf(w_ffn1_out[0]), bf(w_pqkv))

    o, lse = zip(*[_dilated_attention(t) for t in kvq])

    w_pool_bd = jnp.zeros((POOL_WIDTH, POOL_WIDTH), F32)
    for gi in range(len(POOL_WINDOWS)):
        sl = slice(gi * POOL_GROUP_DIM, (gi + 1) * POOL_GROUP_DIM)
        w_pool_bd = w_pool_bd.at[sl, sl].set(w_pool[0, gi])

    return _merge_ffn2(h1, p, o, lse, mod, row(g_norm_mix[0]), row(g_norm_ffn2[0]), row(g_final),
                       bf(w_in0[:, PQKV_WIDTH:]), bf(w_pool_bd), row(pool_scale[0]),
                       bf(w_pool_branch[0]), bf(w_attn_branch[0]), bf(w_out[0]),
                       bf(w_ffn2_in[0]), bf(w_ffn2_out[0]))
```

```python
import jax
import jax.numpy as jnp
from jax import lax
from jax.experimental import pallas as pl
from jax.experimental.pallas import tpu as pltpu

F32 = jnp.float32
BF16 = jnp.bfloat16

D_MODEL = 1024
POOL_WINDOWS = (2, 4, 8, 16)
POOL_GROUP_DIM = 64
POOL_WIDTH = 256
ATTN_DILATIONS = (1, 4, 16)
N_GROUPS = 3
HEAD_DIM = 64
GROUP_WIDTH = 256
ATTN_WIDTH = N_GROUPS * GROUP_WIDTH
PQKV_WIDTH = POOL_WIDTH + 3 * ATTN_WIDTH
KVQ_WIDTH = 3 * GROUP_WIDTH
D_FF = 2816
N_MOD = 9
ROPE_THETA = 10000.0
EPS = 1e-6
NEG_BIG = -1e30

LANES = 128
VMEM_LIMIT_BYTES = 60 * 1024 * 1024

TOKEN_TILE = 512
FF_CHUNKS = ((0, 1024), (1024, 2048), (2048, D_FF))
ATTN_ROWS = 1024
QBLK = 128
POOL_HALO = 16
SLABS = GROUP_WIDTH // LANES
ROW_PARTS = 2
GATE_CHUNKS = 2
FINAL_PARTS = 4
WIDE_PITCH = 40
VEC_ROWS = 64


def _resident(shape):
    nd = len(shape)
    return pl.BlockSpec(shape, lambda *_: (0,) * nd, pipeline_mode=pl.Buffered(1))


def _rms_modulate(x, g, shift, scale):
    ms = jnp.mean(x * x, axis=-1, keepdims=True)
    return x * lax.rsqrt(ms + EPS) * (g * (1.0 + scale)) + shift


def _dot(a, b):
    return jnp.dot(a, b, preferred_element_type=F32)


def _sigmoid(x):
    return 0.5 * jnp.tanh(0.5 * x) + 0.5


def _row_parts(x, parts):
    n = x.shape[0] // parts
    return [x[i * n:(i + 1) * n] for i in range(parts)]


def _dot_parts(parts, w):
    return jnp.concatenate([_dot(part, w) for part in parts], axis=0)


def _swiglu(u_parts, w_in_ref, w_out_ref, out_parts):
    u = jnp.concatenate(u_parts, axis=0)

    def up(idx):
        c0, c1 = FF_CHUNKS[idx]
        wa, wb = w_in_ref[:, c0:c1], w_in_ref[:, D_FF + c0:D_FF + c1]
        if idx == 0:
            return _dot_parts(u_parts, wa), _dot_parts(u_parts, wb)
        return _dot(u, wa), _dot(u, wb)

    y = None
    pending = up(0)
    last = len(FF_CHUNKS) - 1
    for idx, (c0, c1) in enumerate(FF_CHUNKS):
        following = up(idx + 1) if idx < last else None
        a, b = pending
        act = (a * _sigmoid(a) * b).astype(BF16)
        if idx == last:
            return [yp + _dot(ap, w_out_ref[c0:c1, :])
                    for yp, ap in zip(_row_parts(y, out_parts), _row_parts(act, out_parts))]
        part = _dot(act, w_out_ref[c0:c1, :])
        y = part if y is None else y + part
        pending = following


def _ada_kernel(c_ref, w_ref, b_ref, o_ref):
    c = c_ref[...]
    cond = c * jax.nn.sigmoid(c)
    c_hi = cond.astype(BF16)
    c_lo = (cond - c_hi.astype(F32)).astype(BF16)
    w = w_ref[...]
    w_hi = w.astype(BF16)
    w_lo = (w - w_hi.astype(F32)).astype(BF16)
    acc = jnp.dot(c_hi, w_hi, preferred_element_type=F32)
    acc += jnp.dot(c_hi, w_lo, preferred_element_type=F32)
    acc += jnp.dot(c_lo, w_hi, preferred_element_type=F32)
    o_ref[...] = acc + b_ref[...]


def _ada_modulation(c, w_ada, b_ada):
    batch = c.shape[0]
    return pl.pallas_call(
        _ada_kernel,
        grid=(N_MOD,),
        in_specs=[
            pl.BlockSpec((batch, D_MODEL), lambda j: (0, 0)),
            pl.BlockSpec((D_MODEL, D_MODEL), lambda j: (0, j)),
            pl.BlockSpec((1, D_MODEL), lambda j: (0, j)),
        ],
        out_specs=pl.BlockSpec((batch, D_MODEL), lambda j: (0, j)),
        out_shape=jax.ShapeDtypeStruct((batch, N_MOD * D_MODEL), F32),
        compiler_params=pltpu.CompilerParams(dimension_semantics=("parallel",)),
        name="ada_modulation",
    )(c, w_ada, b_ada.reshape(1, N_MOD * D_MODEL))


def _ffn1_proj_kernel(x_ref, mod_ref, g1_ref, g2_ref, pos_ref, invf_ref,
                      w1i_ref, w1o_ref, wp_ref,
                      h_ref, p_ref, kvq0_ref, kvq1_ref, kvq2_ref, trig_ref, stage_ref):
    tm = x_ref.shape[1]
    n_chunks = tm // QBLK
    half = HEAD_DIM // 2
    assert n_chunks * half == LANES

    lane = lax.broadcasted_iota(jnp.int32, (QBLK, LANES), 1)
    quarter = lane >> 5
    first_half = (lane & (HEAD_DIM - 1)) < half
    pos_c = None
    for j in range(n_chunks):
        pos_row = pos_ref[0, 0, j:j + 1, :].astype(F32)
        pos_col = jnp.broadcast_to(pos_row, (QBLK, LANES)).T
        pos_c = pos_col if j == 0 else jnp.where(quarter == j, pos_col, pos_c)
    ang = pos_c * invf_ref[...]
    for t, tbl in enumerate((jnp.cos(ang), jnp.sin(ang))):
        rolled = [tbl] + [pltpu.roll(tbl, half * k, 1) for k in range(1, n_chunks)]
        for j in range(n_chunks):
            full = rolled[(-j) % n_chunks]
            for ql in range(1, n_chunks):
                full = jnp.where(quarter == ql, rolled[(ql - j) % n_chunks], full)
            if t == 1:
                full = jnp.where(first_half, -full, full)
            trig_ref[t, j * QBLK:(j + 1) * QBLK, :] = full

    x = x_ref[0]
    mod = mod_ref[0]
    u = _rms_modulate(x, g1_ref[...], mod[0:1], mod[1:2]).astype(BF16)
    h = x + (0.5 * mod[2:3]) * _swiglu([u], w1i_ref, w1o_ref, 1)[0]
    h_ref[0] = h
    u2 = _rms_modulate(h, g2_ref[...], mod[3:4], mod[4:5]).astype(BF16)

    qbase, kbase, vbase = POOL_WIDTH, POOL_WIDTH + ATTN_WIDTH, POOL_WIDTH + 2 * ATTN_WIDTH
    staged = (N_GROUPS - 1) * SLABS

    def emit(val, which, g, sl):
        if g == 0:
            kvq0_ref[0, 0, :, which * GROUP_WIDTH + sl * LANES:which * GROUP_WIDTH + (sl + 1) * LANES] = val.astype(BF16)
        else:
            stage_ref[which * staged + (g - 1) * SLABS + sl] = val

    def rotary(t):
        rot = jnp.where(jnp.tile(first_half, (n_chunks, 1)), pltpu.roll(t, LANES - half, 1), pltpu.roll(t, half, 1))
        return t * trig_ref[0] + rot * trig_ref[1]

    pk = _dot(u2, wp_ref[:, kbase:kbase + ATTN_WIDTH])
    pq = _dot(u2, wp_ref[:, qbase:qbase + ATTN_WIDTH])
    for s in range(N_GROUPS * SLABS):
        emit(rotary(pk[:, s * LANES:(s + 1) * LANES]), 0, s // SLABS, s % SLABS)
    p_ref[0] = _dot(u2, wp_ref[:, :POOL_WIDTH])
    pv = _dot(u2, wp_ref[:, vbase:vbase + ATTN_WIDTH])
    for s in range(N_GROUPS * SLABS):
        emit(rotary(pq[:, s * LANES:(s + 1) * LANES]), 2, s // SLABS, s % SLABS)
    for s in range(N_GROUPS * SLABS):
        emit(pv[:, s * LANES:(s + 1) * LANES], 1, s // SLABS, s % SLABS)

    for g, out_ref in ((1, kvq1_ref), (2, kvq2_ref)):
        d = ATTN_DILATIONS[g]
        for which in range(3):
            for sl in range(SLABS):
                s = which * staged + (g - 1) * SLABS + sl
                cols = slice(which * GROUP_WIDTH + sl * LANES, which * GROUP_WIDTH + (sl + 1) * LANES)
                for r in range(d):
                    out_ref[0, r, :, cols] = stage_ref[s, pl.ds(r, tm // d, stride=d), :].astype(BF16)


def _ffn1_proj(x, mod, g1, g2, pos4, invf, w1i, w1o, wp):
    batch, seq, _ = x.shape
    tm = TOKEN_TILE
    tile = lambda w: pl.BlockSpec((1, tm, w), lambda b, i: (b, i, 0))
    kvq_specs = [pl.BlockSpec((1, d, tm // d, KVQ_WIDTH), lambda b, i: (b, 0, i, 0)) for d in ATTN_DILATIONS]
    kvq_shapes = [jax.ShapeDtypeStruct((batch, d, seq // d, KVQ_WIDTH), BF16) for d in ATTN_DILATIONS]
    return pl.pallas_call(
        _ffn1_proj_kernel,
        grid=(batch, seq // tm),
        in_specs=[
            tile(D_MODEL),
            pl.BlockSpec((1, N_MOD, D_MODEL), lambda b, i: (b, 0, 0)),
            _resident((1, D_MODEL)),
            _resident((1, D_MODEL)),
            pl.BlockSpec((1, 1, tm // LANES, LANES), lambda b, i: (b, i, 0, 0)),
            _resident((1, LANES)),
            _resident(w1i.shape),
            _resident(w1o.shape),
            _resident(wp.shape),
        ],
        out_specs=[tile(D_MODEL), tile(POOL_WIDTH)] + kvq_specs,
        out_shape=[
            jax.ShapeDtypeStruct((batch, seq, D_MODEL), F32),
            jax.ShapeDtypeStruct((batch, seq, POOL_WIDTH), F32),
        ] + kvq_shapes,
        scratch_shapes=[pltpu.VMEM((2, tm, LANES), F32),
                        pltpu.VMEM((3 * (N_GROUPS - 1) * SLABS, tm, LANES), F32)],
        compiler_params=pltpu.CompilerParams(
            dimension_semantics=("parallel", "parallel"),
            vmem_limit_bytes=VMEM_LIMIT_BYTES),
        name="ffn1_proj",
    )(x, mod, g1, g2, pos4, invf, w1i, w1o, wp)


def _attn_kernel(cur_ref, prev_ref, o_ref, lse_ref):
    n_res = cur_ref.shape[1]
    nq = cur_ref.shape[2] // QBLK
    has_prev = pl.program_id(2) > 0

    a = lax.broadcasted_iota(jnp.int32, (2 * QBLK, 2 * QBLK), 0) & (QBLK - 1)
    c = lax.broadcasted_iota(jnp.int32, (2 * QBLK, 2 * QBLK), 1)
    bias = jnp.where((c >= a) & (c <= a + QBLK), 0.0, NEG_BIG).astype(F32)
    bias_first = jnp.where((c >= QBLK) | has_prev, bias, NEG_BIG)
    head0 = lax.broadcasted_iota(jnp.int32, (QBLK, LANES), 1) < HEAD_DIM
    ones = jnp.ones((2 * QBLK, LANES), BF16)
    dn = (((1,), (1,)), ((), ()))

    def band(r, j, cols):
        if j == 0:
            return jnp.concatenate([prev_ref[0, r, :, cols], cur_ref[0, r, :QBLK, cols]], axis=0)
        return cur_ref[0, r, (j - 1) * QBLK:(j + 1) * QBLK, cols]

    blocks = [(r, sl, j) for r in range(n_res) for sl in range(SLABS) for j in range(nq)]
    scores = []
    for r, sl, j in blocks:
        qs = cur_ref[0, r, j * QBLK:(j + 1) * QBLK, 2 * GROUP_WIDTH + sl * LANES:2 * GROUP_WIDTH + (sl + 1) * LANES]
        zero = jnp.zeros_like(qs)
        lhs = jnp.concatenate([jnp.where(head0, qs, zero), jnp.where(head0, zero, qs)], axis=0)
        s = lax.dot_general(lhs, band(r, j, slice(sl * LANES, (sl + 1) * LANES)), dn, preferred_element_type=F32)
        scores.append(s + (bias_first if j == 0 else bias))
    probs = []
    for s in scores:
        m = jnp.max(s, axis=-1, keepdims=True)
        probs.append((jnp.exp(s - m).astype(BF16), m))
    for (r, sl, j), (p, m) in zip(blocks, probs):
        vband = band(r, j, slice(GROUP_WIDTH + sl * LANES, GROUP_WIDTH + (sl + 1) * LANES))
        out = jnp.dot(p, jnp.concatenate([vband, ones], axis=1), preferred_element_type=F32)
        num = jnp.where(head0, out[:QBLK, :LANES], out[QBLK:, :LANES])
        den = jnp.where(head0, out[:QBLK, LANES:], out[QBLK:, LANES:])
        mm = jnp.where(head0, jnp.broadcast_to(m[:QBLK], (QBLK, LANES)), jnp.broadcast_to(m[QBLK:], (QBLK, LANES)))
        rows = slice(j * QBLK, (j + 1) * QBLK)
        ocols = slice(sl * LANES, (sl + 1) * LANES)
        o_ref[0, r, rows, ocols] = num / den
        lse_ref[0, r, rows, ocols] = mm + jnp.log(den)


def _dilated_attention(kvq):
    batch, d, length, _ = kvq.shape
    n = min(ATTN_ROWS, length)
    n_res = ATTN_ROWS // n
    cur = pl.BlockSpec((1, n_res, n, KVQ_WIDTH), lambda b, r, i: (b, r, i, 0))
    prev = pl.BlockSpec((1, n_res, QBLK, 2 * GROUP_WIDTH),
                        lambda b, r, i: (b, r, jnp.maximum(i * (n // QBLK) - 1, 0), 0))
    out = pl.BlockSpec((1, n_res, n, GROUP_WIDTH), lambda b, r, i: (b, r, i, 0))
    out_shape = jax.ShapeDtypeStruct((batch, d, length, GROUP_WIDTH), F32)
    return pl.pallas_call(
        _attn_kernel,
        grid=(batch, d // n_res, length // n),
        in_specs=[cur, prev],
        out_specs=[out, out],
        out_shape=[out_shape, out_shape],
        compiler_params=pltpu.CompilerParams(
            dimension_semantics=("parallel", "parallel", "parallel"),
            vmem_limit_bytes=VMEM_LIMIT_BYTES),
        name=f"dilated_attn_d{d}",
    )(kvq, kvq)


def _merge_ffn2_kernel(h_ref, p_ref, pprev_ref, o0_ref, o1_ref, o2_ref, l0_ref, l1_ref, l2_ref,
                       mod_ref, gmix_ref, g3_ref, gfin_ref,
                       wg_ref, wpool_ref, pscale_ref, wpb_ref, wab_ref, wout_ref, w2i_ref, w2o_ref,
                       out_ref, tok_ref, wide_ref, xs_ref, dpool_ref, yattn_ref):
    tm = h_ref.shape[1]
    i = pl.program_id(1)
    mod = mod_ref[0]
    h_parts = _row_parts(h_ref[0], ROW_PARTS)
    u_parts = [_rms_modulate(hp, gmix_ref[...], mod[3:4], mod[4:5]).astype(BF16) for hp in h_parts]
    u = jnp.concatenate(u_parts, axis=0)
    z_pool = _dot_parts(u_parts, wg_ref[:, :D_MODEL])

    xs_ref[:POOL_HALO] = jnp.where(i > 0, pprev_ref[0], 0.0)
    xs_ref[POOL_HALO:] = p_ref[0]
    lane = lax.broadcasted_iota(jnp.int32, (VEC_ROWS, POOL_WIDTH), 1)
    row = lax.broadcasted_iota(jnp.int32, (VEC_ROWS, POOL_WIDTH), 0)
    for r0 in range(0, tm, VEC_ROWS):
        sums = {1: xs_ref[r0:r0 + VEC_ROWS + POOL_HALO]}
        w = 1
        while w < POOL_WINDOWS[-1]:
            t = sums[w]
            sums[2 * w] = t[w:] + t[:-w]
            w *= 2
        trail = jnp.zeros((VEC_ROWS, POOL_WIDTH), F32)
        win = jnp.zeros((VEC_ROWS, POOL_WIDTH), jnp.int32)
        for gi, wnd in enumerate(POOL_WINDOWS):
            in_group = (lane >> 6) == gi
            start = POOL_HALO - (wnd - 1)
            trail = jnp.where(in_group, sums[wnd][start:start + VEC_ROWS], trail)
            win = jnp.where(in_group, wnd, win)
        count = jnp.minimum(row + (i * tm + r0 + 1), win).astype(F32)
        dpool_ref[r0:r0 + VEC_ROWS] = (trail / count - sums[1][POOL_HALO:]).astype(BF16)
    y_pool = (_dot(dpool_ref[...], wpool_ref[...]) * pscale_ref[...]).astype(BF16)
    b_pool = _dot(y_pool, wpb_ref[...])

    gw = D_MODEL // GATE_CHUNKS
    z_attn = [_dot(u, wg_ref[:, D_MODEL + c * gw:D_MODEL + (c + 1) * gw]) for c in range(GATE_CHUNKS)]

    d1, d2 = ATTN_DILATIONS[1], ATTN_DILATIONS[2]
    for a, ref in enumerate((o1_ref, l1_ref)):
        for sl in range(SLABS):
            for r in range(d1):
                tok_ref[a * SLABS + sl, pl.ds(r, tm // d1, stride=d1), :] = ref[0, r, :, sl * LANES:(sl + 1) * LANES]
    for a, ref in enumerate((o2_ref, l2_ref)):
        for sl in range(SLABS):
            for r in range(d2):
                wide_ref[a * SLABS + sl, r * WIDE_PITCH:r * WIDE_PITCH + tm // d2, :] = \
                    ref[0, r, :, sl * LANES:(sl + 1) * LANES]

    def wide_rows(a, sl, r0):
        return jnp.concatenate([wide_ref[a * SLABS + sl, pl.ds(m, d2, stride=WIDE_PITCH), :]
                                for m in range(r0 // d2, (r0 + VEC_ROWS) // d2)], axis=0)

    for sl in range(SLABS):
        cols = slice(sl * LANES, (sl + 1) * LANES)
        for r0 in range(0, tm, VEC_ROWS):
            rows = slice(r0, r0 + VEC_ROWS)
            o0, l0 = o0_ref[0, 0, rows, cols], l0_ref[0, 0, rows, cols]
            o1, l1 = tok_ref[0 * SLABS + sl, rows, :], tok_ref[1 * SLABS + sl, rows, :]
            o2, l2 = wide_rows(0, sl, r0), wide_rows(1, sl, r0)
            mx = jnp.maximum(jnp.maximum(l0, l1), l2)
            e0, e1, e2 = jnp.exp(l0 - mx), jnp.exp(l1 - mx), jnp.exp(l2 - mx)
            yattn_ref[rows, cols] = ((e0 * o0 + e1 * o1 + e2 * o2) / (e0 + e1 + e2)).astype(BF16)
    y_attn = yattn_ref[...]
    b_attn = _dot(y_attn, wab_ref[...])

    gated_pool = _sigmoid(z_pool) * b_pool
    mix = None
    for c in range(GATE_CHUNKS):
        cols = slice(c * gw, (c + 1) * gw)
        merged = (gated_pool[:, cols] + _sigmoid(z_attn[c]) * b_attn[:, cols]).astype(BF16)
        w_rows = wout_ref[cols, :]
        if c + 1 < GATE_CHUNKS:
            part = _dot(merged, w_rows)
            mix = part if mix is None else mix + part
        else:
            mix_parts = [mp + _dot(gp, w_rows)
                         for mp, gp in zip(_row_parts(mix, ROW_PARTS), _row_parts(merged, ROW_PARTS))]
    h_parts = [hp + mod[5:6] * mp for hp, mp in zip(h_parts, mix_parts)]

    u3_parts = [_rms_modulate(hp, g3_ref[...], mod[6:7], mod[7:8]).astype(BF16) for hp in h_parts]
    y_parts = _swiglu(u3_parts, w2i_ref, w2o_ref, FINAL_PARTS)
    n = tm // FINAL_PARTS
    hf_parts = _row_parts(jnp.concatenate(h_parts, axis=0), FINAL_PARTS)
    for part, (hp, yp) in enumerate(zip(hf_parts, y_parts)):
        hf = hp + (0.5 * mod[8:9]) * yp
        ms = jnp.mean(hf * hf, axis=-1, keepdims=True)
        out_ref[0, part * n:(part + 1) * n, :] = hf * lax.rsqrt(ms + EPS) * gfin_ref[...]


def _merge_ffn2(h1, p, o, lse, mod, gmix, g3, gfin, wg, wpool, pscale, wpb, wab, wout, w2i, w2o):
    batch, seq, _ = h1.shape
    tm = TOKEN_TILE
    tile = lambda w: pl.BlockSpec((1, tm, w), lambda b, i: (b, i, 0))
    halo = pl.BlockSpec((1, POOL_HALO, POOL_WIDTH),
                        lambda b, i: (b, jnp.maximum(i * (tm // POOL_HALO) - 1, 0), 0))
    grp = [pl.BlockSpec((1, d, tm // d, GROUP_WIDTH), lambda b, i: (b, 0, i, 0)) for d in ATTN_DILATIONS]
    vec = _resident((1, D_MODEL))
    return pl.pallas_call(
        _merge_ffn2_kernel,
        grid=(batch, seq // tm),
        in_specs=[
            tile(D_MODEL), tile(POOL_WIDTH), halo,
            *grp, *grp,
            pl.BlockSpec((1, N_MOD, D_MODEL), lambda b, i: (b, 0, 0)),
            vec, vec, vec,
            _resident(wg.shape), _resident(wpool.shape), _resident(pscale.shape),
            _resident(wpb.shape), _resident(wab.shape), _resident(wout.shape),
            _resident(w2i.shape), _resident(w2o.shape),
        ],
        out_specs=tile(D_MODEL),
        out_shape=jax.ShapeDtypeStruct((batch, seq, D_MODEL), F32),
        scratch_shapes=[pltpu.VMEM((2 * SLABS, tm, LANES), F32),
                        pltpu.VMEM((2 * SLABS, ATTN_DILATIONS[2] * WIDE_PITCH, LANES), F32),
                        pltpu.VMEM((tm + POOL_HALO, POOL_WIDTH), F32),
                        pltpu.VMEM((tm, POOL_WIDTH), BF16),
                        pltpu.VMEM((tm, GROUP_WIDTH), BF16)],
        compiler_params=pltpu.CompilerParams(
            dimension_semantics=("parallel", "parallel"),
            vmem_limit_bytes=VMEM_LIMIT_BYTES),
        name="merge_ffn2",
    )(h1, p, p, *o, *lse, mod, gmix, g3, gfin, wg, wpool, pscale, wpb, wab, wout, w2i, w2o)


def kernel(x, c, positions, w_ada, b_ada, g_norm_ffn1, w_ffn1_in, w_ffn1_out, g_norm_mix, w_in, w_pool,
           pool_scale, w_pool_branch, w_attn_branch, w_out, g_norm_ffn2, w_ffn2_in, w_ffn2_out, g_final):
    assert w_ada.shape[0] == 1, "single-layer block"
    batch, seq, d_model = x.shape
    assert d_model == D_MODEL and seq % TOKEN_TILE == 0
    assert all((seq // d) % QBLK == 0 and ATTN_ROWS % min(ATTN_ROWS, seq // d) == 0 for d in ATTN_DILATIONS)

    mod = _ada_modulation(c, w_ada[0], b_ada[0]).reshape(batch, N_MOD, D_MODEL)

    row = lambda g: g.reshape(1, -1)
    bf = lambda w: w.astype(BF16)
    freq = ROPE_THETA ** (-jnp.arange(0, HEAD_DIM, 2, dtype=F32) / HEAD_DIM)
    invf = jnp.tile(freq, LANES // (HEAD_DIM // 2)).reshape(1, LANES)
    pos4 = positions.reshape(batch, seq // TOKEN_TILE, TOKEN_TILE // LANES, LANES)

    col = jnp.arange(PQKV_WIDTH)
    q_cols = (col >= POOL_WIDTH) & (col < POOL_WIDTH + ATTN_WIDTH)
    w_in0 = w_in[0]
    w_pqkv = w_in0[:, :PQKV_WIDTH] * jnp.where(q_cols, HEAD_DIM ** -0.5, 1.0).astype(F32)
    h1, p, *kvq = _ffn1_proj(x, mod, row(g_norm_ffn1[0]), row(g_norm_mix[0]), pos4, invf,
                             bf(w_ffn1_in[0]), bf(w_ffn1_out[0]), bf(w_pqkv))

    o, lse = zip(*[_dilated_attention(t) for t in kvq])

    w_pool_bd = jnp.zeros((POOL_WIDTH, POOL_WIDTH), F32)
    for gi in range(len(POOL_WINDOWS)):
        sl = slice(gi * POOL_GROUP_DIM, (gi + 1) * POOL_GROUP_DIM)
        w_pool_bd = w_pool_bd.at[sl, sl].set(w_pool[0, gi])

    return _merge_ffn2(h1, p, o, lse, mod, row(g_norm_mix[0]), row(g_norm_ffn2[0]), row(g_final),
                       bf(w_in0[:, PQKV_WIDTH:]), bf(w_pool_bd), row(pool_scale[0]),
                       bf(w_pool_branch[0]), bf(w_attn_branch[0]), bf(w_out[0]),
                       bf(w_ffn2_in[0]), bf(w_ffn2_out[0]))
```

```python
import jax
import jax.numpy as jnp
from jax import lax
from jax.experimental import pallas as pl
from jax.experimental.pallas import tpu as pltpu

F32 = jnp.float32
BF16 = jnp.bfloat16

D_MODEL = 1024
POOL_WINDOWS = (2, 4, 8, 16)
POOL_GROUP_DIM = 64
POOL_WIDTH = 256
ATTN_DILATIONS = (1, 4, 16)
N_GROUPS = 3
HEAD_DIM = 64
GROUP_WIDTH = 256
ATTN_WIDTH = N_GROUPS * GROUP_WIDTH
PQKV_WIDTH = POOL_WIDTH + 3 * ATTN_WIDTH
KVQ_WIDTH = 3 * GROUP_WIDTH
D_FF = 2816
N_MOD = 9
ROPE_THETA = 10000.0
EPS = 1e-6
NEG_BIG = -1e30

LANES = 128
VMEM_LIMIT_BYTES = 60 * 1024 * 1024

TOKEN_TILE = 512
FF_CHUNKS = ((0, 1024), (1024, 2048), (2048, D_FF))
ATTN_ROWS = 2048
QBLK = 128
POOL_HALO = 16
SLABS = GROUP_WIDTH // LANES
ROW_PARTS = 2
GATE_CHUNKS = 2
FIRST_PARTS = 4
FINAL_PARTS = 4
WIDE_PITCH = 40
VEC_ROWS = 64


def _resident(shape):
    nd = len(shape)
    return pl.BlockSpec(shape, lambda *_: (0,) * nd, pipeline_mode=pl.Buffered(1))


def _rms_modulate(x, g, shift, scale):
    ms = jnp.mean(x * x, axis=-1, keepdims=True)
    return x * lax.rsqrt(ms + EPS) * (g * (1.0 + scale)) + shift


def _dot(a, b):
    return jnp.dot(a, b, preferred_element_type=F32)


def _sigmoid(x):
    return 0.5 * jnp.tanh(0.5 * x) + 0.5


def _row_parts(x, parts):
    n = x.shape[0] // parts
    return [x[i * n:(i + 1) * n] for i in range(parts)]


def _dot_parts(parts, w):
    return jnp.concatenate([_dot(part, w) for part in parts], axis=0)


def _swiglu(u_parts, w_in_ref, w_out_ref, out_parts):
    u = jnp.concatenate(u_parts, axis=0)

    def up(idx):
        c0, c1 = FF_CHUNKS[idx]
        wa, wb = w_in_ref[:, c0:c1], w_in_ref[:, D_FF + c0:D_FF + c1]
        if idx == 0:
            return _dot_parts(u_parts, wa), _dot_parts(u_parts, wb)
        return _dot(u, wa), _dot(u, wb)

    y = None
    pending = up(0)
    last = len(FF_CHUNKS) - 1
    for idx, (c0, c1) in enumerate(FF_CHUNKS):
        following = up(idx + 1) if idx < last else None
        a, b = pending
        act = (a * _sigmoid(a) * b).astype(BF16)
        if idx == last:
            return [yp + _dot(ap, w_out_ref[c0:c1, :])
                    for yp, ap in zip(_row_parts(y, out_parts), _row_parts(act, out_parts))]
        part = _dot(act, w_out_ref[c0:c1, :])
        y = part if y is None else y + part
        pending = following


def _ada_kernel(c_ref, w_ref, b_ref, o_ref):
    c = c_ref[...]
    cond = c * jax.nn.sigmoid(c)
    c_hi = cond.astype(BF16)
    c_lo = (cond - c_hi.astype(F32)).astype(BF16)
    w = w_ref[...]
    w_hi = w.astype(BF16)
    w_lo = (w - w_hi.astype(F32)).astype(BF16)
    acc = jnp.dot(c_hi, w_hi, preferred_element_type=F32)
    acc += jnp.dot(c_hi, w_lo, preferred_element_type=F32)
    acc += jnp.dot(c_lo, w_hi, preferred_element_type=F32)
    o_ref[...] = acc + b_ref[...]


def _ada_modulation(c, w_ada, b_ada):
    batch = c.shape[0]
    return pl.pallas_call(
        _ada_kernel,
        grid=(N_MOD,),
        in_specs=[
            pl.BlockSpec((batch, D_MODEL), lambda j: (0, 0)),
            pl.BlockSpec((D_MODEL, D_MODEL), lambda j: (0, j)),
            pl.BlockSpec((1, D_MODEL), lambda j: (0, j)),
        ],
        out_specs=pl.BlockSpec((batch, D_MODEL), lambda j: (0, j)),
        out_shape=jax.ShapeDtypeStruct((batch, N_MOD * D_MODEL), F32),
        compiler_params=pltpu.CompilerParams(dimension_semantics=("parallel",)),
        name="ada_modulation",
    )(c, w_ada, b_ada.reshape(1, N_MOD * D_MODEL))


def _ffn1_proj_kernel(x_ref, mod_ref, g1_ref, g2_ref, pos_ref, invf_ref,
                      w1i_ref, w1o_ref, wp_ref,
                      h_ref, p_ref, kvq0_ref, kvq1_ref, kvq2_ref, trig_ref, stage_ref):
    tm = x_ref.shape[1]
    n_chunks = tm // QBLK
    half = HEAD_DIM // 2
    assert n_chunks * half == LANES

    lane = lax.broadcasted_iota(jnp.int32, (QBLK, LANES), 1)
    quarter = lane >> 5
    first_half = (lane & (HEAD_DIM - 1)) < half
    pos_c = None
    for j in range(n_chunks):
        pos_row = pos_ref[0, 0, j:j + 1, :].astype(F32)
        pos_col = jnp.broadcast_to(pos_row, (QBLK, LANES)).T
        pos_c = pos_col if j == 0 else jnp.where(quarter == j, pos_col, pos_c)
    ang = pos_c * invf_ref[...]
    for t, tbl in enumerate((jnp.cos(ang), jnp.sin(ang))):
        rolled = [tbl] + [pltpu.roll(tbl, half * k, 1) for k in range(1, n_chunks)]
        for j in range(n_chunks):
            full = rolled[(-j) % n_chunks]
            for ql in range(1, n_chunks):
                full = jnp.where(quarter == ql, rolled[(ql - j) % n_chunks], full)
            if t == 1:
                full = jnp.where(first_half, -full, full)
            trig_ref[t, j * QBLK:(j + 1) * QBLK, :] = full

    x = x_ref[0]
    mod = mod_ref[0]
    u_parts = [_rms_modulate(xp, g1_ref[...], mod[0:1], mod[1:2]).astype(BF16) for xp in _row_parts(x, FIRST_PARTS)]
    h = x + (0.5 * mod[2:3]) * _swiglu(u_parts, w1i_ref, w1o_ref, 1)[0]
    h_ref[0] = h
    u2 = _rms_modulate(h, g2_ref[...], mod[3:4], mod[4:5]).astype(BF16)

    qbase, kbase, vbase = POOL_WIDTH, POOL_WIDTH + ATTN_WIDTH, POOL_WIDTH + 2 * ATTN_WIDTH
    staged = (N_GROUPS - 1) * SLABS

    def emit(val, which, g, sl):
        if g == 0:
            kvq0_ref[0, 0, :, which * GROUP_WIDTH + sl * LANES:which * GROUP_WIDTH + (sl + 1) * LANES] = val.astype(BF16)
        else:
            stage_ref[which * staged + (g - 1) * SLABS + sl] = val

    def rotary(t):
        rot = jnp.where(jnp.tile(first_half, (n_chunks, 1)), pltpu.roll(t, LANES - half, 1), pltpu.roll(t, half, 1))
        return t * trig_ref[0] + rot * trig_ref[1]

    pk = _dot(u2, wp_ref[:, kbase:kbase + ATTN_WIDTH])
    pq = _dot(u2, wp_ref[:, qbase:qbase + ATTN_WIDTH])
    for s in range(N_GROUPS * SLABS):
        emit(rotary(pk[:, s * LANES:(s + 1) * LANES]), 0, s // SLABS, s % SLABS)
    pv = _dot(u2, wp_ref[:, vbase:vbase + ATTN_WIDTH])
    for s in range(N_GROUPS * SLABS):
        emit(rotary(pq[:, s * LANES:(s + 1) * LANES]), 2, s // SLABS, s % SLABS)
    pp = _dot(u2, wp_ref[:, :POOL_WIDTH])
    for s in range(N_GROUPS * SLABS):
        emit(pv[:, s * LANES:(s + 1) * LANES], 1, s // SLABS, s % SLABS)

    for g, out_ref in ((1, kvq1_ref), (2, kvq2_ref)):
        d = ATTN_DILATIONS[g]
        for which in range(3):
            for sl in range(SLABS):
                s = which * staged + (g - 1) * SLABS + sl
                cols = slice(which * GROUP_WIDTH + sl * LANES, which * GROUP_WIDTH + (sl + 1) * LANES)
                for r in range(d):
                    out_ref[0, r, :, cols] = stage_ref[s, pl.ds(r, tm // d, stride=d), :].astype(BF16)
    p_ref[0] = pp


def _ffn1_proj(x, mod, g1, g2, pos4, invf, w1i, w1o, wp):
    batch, seq, _ = x.shape
    tm = TOKEN_TILE
    tile = lambda w: pl.BlockSpec((1, tm, w), lambda b, i: (b, i, 0))
    kvq_specs = [pl.BlockSpec((1, d, tm // d, KVQ_WIDTH), lambda b, i: (b, 0, i, 0)) for d in ATTN_DILATIONS]
    kvq_shapes = [jax.ShapeDtypeStruct((batch, d, seq // d, KVQ_WIDTH), BF16) for d in ATTN_DILATIONS]
    return pl.pallas_call(
        _ffn1_proj_kernel,
        grid=(batch, seq // tm),
        in_specs=[
            tile(D_MODEL),
            pl.BlockSpec((1, N_MOD, D_MODEL), lambda b, i: (b, 0, 0)),
            _resident((1, D_MODEL)),
            _resident((1, D_MODEL)),
            pl.BlockSpec((1, 1, tm // LANES, LANES), lambda b, i: (b, i, 0, 0)),
            _resident((1, LANES)),
            _resident(w1i.shape),
            _resident(w1o.shape),
            _resident(wp.shape),
        ],
        out_specs=[tile(D_MODEL), tile(POOL_WIDTH)] + kvq_specs,
        out_shape=[
            jax.ShapeDtypeStruct((batch, seq, D_MODEL), F32),
            jax.ShapeDtypeStruct((batch, seq, POOL_WIDTH), F32),
        ] + kvq_shapes,
        scratch_shapes=[pltpu.VMEM((2, tm, LANES), F32),
                        pltpu.VMEM((3 * (N_GROUPS - 1) * SLABS, tm, LANES), F32)],
        compiler_params=pltpu.CompilerParams(
            dimension_semantics=("parallel", "parallel"),
            vmem_limit_bytes=VMEM_LIMIT_BYTES),
        name="ffn1_proj",
    )(x, mod, g1, g2, pos4, invf, w1i, w1o, wp)


def _attn_kernel(cur_ref, prev_ref, o_ref, lse_ref):
    n_res = cur_ref.shape[1]
    nq = cur_ref.shape[2] // QBLK
    has_prev = pl.program_id(2) > 0

    a = lax.broadcasted_iota(jnp.int32, (2 * QBLK, 2 * QBLK), 0) & (QBLK - 1)
    c = lax.broadcasted_iota(jnp.int32, (2 * QBLK, 2 * QBLK), 1)
    bias = jnp.where((c >= a) & (c <= a + QBLK), 0.0, NEG_BIG).astype(F32)
    bias_first = jnp.where((c >= QBLK) | has_prev, bias, NEG_BIG)
    head0 = lax.broadcasted_iota(jnp.int32, (QBLK, LANES), 1) < HEAD_DIM
    ones = jnp.ones((2 * QBLK, LANES), BF16)
    dn = (((1,), (1,)), ((), ()))

    def band(r, j, cols):
        if j == 0:
            return jnp.concatenate([prev_ref[0, r, :, cols], cur_ref[0, r, :QBLK, cols]], axis=0)
        return cur_ref[0, r, (j - 1) * QBLK:(j + 1) * QBLK, cols]

    blocks = [(r, sl, j) for r in range(n_res) for sl in range(SLABS) for j in range(nq)]
    scores = []
    for r, sl, j in blocks:
        qs = cur_ref[0, r, j * QBLK:(j + 1) * QBLK, 2 * GROUP_WIDTH + sl * LANES:2 * GROUP_WIDTH + (sl + 1) * LANES]
        zero = jnp.zeros_like(qs)
        lhs = jnp.concatenate([jnp.where(head0, qs, zero), jnp.where(head0, zero, qs)], axis=0)
        s = lax.dot_general(lhs, band(r, j, slice(sl * LANES, (sl + 1) * LANES)), dn, preferred_element_type=F32)
        scores.append(s + (bias_first if j == 0 else bias))
    probs = []
    for s in scores:
        m = jnp.max(s, axis=-1, keepdims=True)
        probs.append((jnp.exp(s - m).astype(BF16), m))
    for (r, sl, j), (p, m) in zip(blocks, probs):
        vband = band(r, j, slice(GROUP_WIDTH + sl * LANES, GROUP_WIDTH + (sl + 1) * LANES))
        out = jnp.dot(p, jnp.concatenate([vband, ones], axis=1), preferred_element_type=F32)
        num = jnp.where(head0, out[:QBLK, :LANES], out[QBLK:, :LANES])
        den = jnp.where(head0, out[:QBLK, LANES:], out[QBLK:, LANES:])
        mm = jnp.where(head0, jnp.broadcast_to(m[:QBLK], (QBLK, LANES)), jnp.broadcast_to(m[QBLK:], (QBLK, LANES)))
        rows = slice(j * QBLK, (j + 1) * QBLK)
        ocols = slice(sl * LANES, (sl + 1) * LANES)
        o_ref[0, r, rows, ocols] = num / den
        lse_ref[0, r, rows, ocols] = mm + jnp.log(den)


def _dilated_attention(kvq):
    batch, d, length, _ = kvq.shape
    n = min(ATTN_ROWS, length)
    n_res = ATTN_ROWS // n
    cur = pl.BlockSpec((1, n_res, n, KVQ_WIDTH), lambda b, r, i: (b, r, i, 0))
    prev = pl.BlockSpec((1, n_res, QBLK, 2 * GROUP_WIDTH),
                        lambda b, r, i: (b, r, jnp.maximum(i * (n // QBLK) - 1, 0), 0))
    out = pl.BlockSpec((1, n_res, n, GROUP_WIDTH), lambda b, r, i: (b, r, i, 0))
    out_shape = jax.ShapeDtypeStruct((batch, d, length, GROUP_WIDTH), F32)
    return pl.pallas_call(
        _attn_kernel,
        grid=(batch, d // n_res, length // n),
        in_specs=[cur, prev],
        out_specs=[out, out],
        out_shape=[out_shape, out_shape],
        compiler_params=pltpu.CompilerParams(
            dimension_semantics=("parallel", "parallel", "parallel"),
            vmem_limit_bytes=VMEM_LIMIT_BYTES),
        name=f"dilated_attn_d{d}",
    )(kvq, kvq)


def _merge_ffn2_kernel(h_ref, p_ref, pprev_ref, o0_ref, o1_ref, o2_ref, l0_ref, l1_ref, l2_ref,
                       mod_ref, gmix_ref, g3_ref, gfin_ref,
                       wg_ref, wpool_ref, pscale_ref, wpb_ref, wab_ref, wout_ref, w2i_ref, w2o_ref,
                       out_ref, tok_ref, wide_ref, xs_ref, dpool_ref, yattn_ref):
    tm = h_ref.shape[1]
    i = pl.program_id(1)
    mod = mod_ref[0]
    h_parts = _row_parts(h_ref[0], ROW_PARTS)
    u_parts = [_rms_modulate(hp, gmix_ref[...], mod[3:4], mod[4:5]).astype(BF16) for hp in h_parts]
    u = jnp.concatenate(u_parts, axis=0)
    z_pool = _dot_parts(u_parts, wg_ref[:, :D_MODEL])

    xs_ref[:POOL_HALO] = jnp.where(i > 0, pprev_ref[0], 0.0)
    xs_ref[POOL_HALO:] = p_ref[0]
    lane = lax.broadcasted_iota(jnp.int32, (VEC_ROWS, POOL_WIDTH), 1)
    row = lax.broadcasted_iota(jnp.int32, (VEC_ROWS, POOL_WIDTH), 0)
    for r0 in range(0, tm, VEC_ROWS):
        sums = {1: xs_ref[r0:r0 + VEC_ROWS + POOL_HALO]}
        w = 1
        while w < POOL_WINDOWS[-1]:
            t = sums[w]
            sums[2 * w] = t[w:] + t[:-w]
            w *= 2
        trail = jnp.zeros((VEC_ROWS, POOL_WIDTH), F32)
        win = jnp.zeros((VEC_ROWS, POOL_WIDTH), jnp.int32)
        for gi, wnd in enumerate(POOL_WINDOWS):
            in_group = (lane >> 6) == gi
            start = POOL_HALO - (wnd - 1)
            trail = jnp.where(in_group, sums[wnd][start:start + VEC_ROWS], trail)
            win = jnp.where(in_group, wnd, win)
        count = jnp.minimum(row + (i * tm + r0 + 1), win).astype(F32)
        dpool_ref[r0:r0 + VEC_ROWS] = (trail / count - sums[1][POOL_HALO:]).astype(BF16)
    y_pool = (_dot(dpool_ref[...], wpool_ref[...]) * pscale_ref[...]).astype(BF16)
    b_pool = _dot(y_pool, wpb_ref[...])

    gw = D_MODEL // GATE_CHUNKS
    z_attn = [_dot(u, wg_ref[:, D_MODEL + c * gw:D_MODEL + (c + 1) * gw]) for c in range(GATE_CHUNKS)]

    d1, d2 = ATTN_DILATIONS[1], ATTN_DILATIONS[2]
    for a, ref in enumerate((o1_ref, l1_ref)):
        for sl in range(SLABS):
            for r in range(d1):
                tok_ref[a * SLABS + sl, pl.ds(r, tm // d1, stride=d1), :] = ref[0, r, :, sl * LANES:(sl + 1) * LANES]
    for a, ref in enumerate((o2_ref, l2_ref)):
        for sl in range(SLABS):
            for r in range(d2):
                wide_ref[a * SLABS + sl, r * WIDE_PITCH:r * WIDE_PITCH + tm // d2, :] = \
                    ref[0, r, :, sl * LANES:(sl + 1) * LANES]

    def wide_rows(a, sl, r0):
        return jnp.concatenate([wide_ref[a * SLABS + sl, pl.ds(m, d2, stride=WIDE_PITCH), :]
                                for m in range(r0 // d2, (r0 + VEC_ROWS) // d2)], axis=0)

    for sl in range(SLABS):
        cols = slice(sl * LANES, (sl + 1) * LANES)
        for r0 in range(0, tm, VEC_ROWS):
            rows = slice(r0, r0 + VEC_ROWS)
            o0, l0 = o0_ref[0, 0, rows, cols], l0_ref[0, 0, rows, cols]
            o1, l1 = tok_ref[0 * SLABS + sl, rows, :], tok_ref[1 * SLABS + sl, rows, :]
            o2, l2 = wide_rows(0, sl, r0), wide_rows(1, sl, r0)
            mx = jnp.maximum(jnp.maximum(l0, l1), l2)
            e0, e1, e2 = jnp.exp(l0 - mx), jnp.exp(l1 - mx), jnp.exp(l2 - mx)
            yattn_ref[rows, cols] = ((e0 * o0 + e1 * o1 + e2 * o2) / (e0 + e1 + e2)).astype(BF16)
    y_attn = yattn_ref[...]
    b_attn = _dot(y_attn, wab_ref[...])

    gated_pool = _sigmoid(z_pool) * b_pool
    mix = None
    for c in range(GATE_CHUNKS):
        cols = slice(c * gw, (c + 1) * gw)
        merged = (gated_pool[:, cols] + _sigmoid(z_attn[c]) * b_attn[:, cols]).astype(BF16)
        w_rows = wout_ref[cols, :]
        if c + 1 < GATE_CHUNKS:
            part = _dot(merged, w_rows)
            mix = part if mix is None else mix + part
        else:
            mix_parts = [mp + _dot(gp, w_rows)
                         for mp, gp in zip(_row_parts(mix, ROW_PARTS), _row_parts(merged, ROW_PARTS))]
    h_parts = [hp + mod[5:6] * mp for hp, mp in zip(h_parts, mix_parts)]

    u3_parts = [_rms_modulate(hp, g3_ref[...], mod[6:7], mod[7:8]).astype(BF16) for hp in h_parts]
    y_parts = _swiglu(u3_parts, w2i_ref, w2o_ref, FINAL_PARTS)
    n = tm // FINAL_PARTS
    hf_parts = _row_parts(jnp.concatenate(h_parts, axis=0), FINAL_PARTS)
    for part, (hp, yp) in enumerate(zip(hf_parts, y_parts)):
        hf = hp + (0.5 * mod[8:9]) * yp
        ms = jnp.mean(hf * hf, axis=-1, keepdims=True)
        out_ref[0, part * n:(part + 1) * n, :] = hf * lax.rsqrt(ms + EPS) * gfin_ref[...]


def _merge_ffn2(h1, p, o, lse, mod, gmix, g3, gfin, wg, wpool, pscale, wpb, wab, wout, w2i, w2o):
    batch, seq, _ = h1.shape
    tm = TOKEN_TILE
    tile = lambda w: pl.BlockSpec((1, tm, w), lambda b, i: (b, i, 0))
    halo = pl.BlockSpec((1, POOL_HALO, POOL_WIDTH),
                        lambda b, i: (b, jnp.maximum(i * (tm // POOL_HALO) - 1, 0), 0))
    grp = [pl.BlockSpec((1, d, tm // d, GROUP_WIDTH), lambda b, i: (b, 0, i, 0)) for d in ATTN_DILATIONS]
    vec = _resident((1, D_MODEL))
    return pl.pallas_call(
        _merge_ffn2_kernel,
        grid=(batch, seq // tm),
        in_specs=[
            tile(D_MODEL), tile(POOL_WIDTH), halo,
            *grp, *grp,
            pl.BlockSpec((1, N_MOD, D_MODEL), lambda b, i: (b, 0, 0)),
            vec, vec, vec,
            _resident(wg.shape), _resident(wpool.shape), _resident(pscale.shape),
            _resident(wpb.shape), _resident(wab.shape), _resident(wout.shape),
            _resident(w2i.shape), _resident(w2o.shape),
        ],
        out_specs=tile(D_MODEL),
        out_shape=jax.ShapeDtypeStruct((batch, seq, D_MODEL), F32),
        scratch_shapes=[pltpu.VMEM((2 * SLABS, tm, LANES), F32),
                        pltpu.VMEM((2 * SLABS, ATTN_DILATIONS[2] * WIDE_PITCH, LANES), F32),
                        pltpu.VMEM((tm + POOL_HALO, POOL_WIDTH), F32),
                        pltpu.VMEM((tm, POOL_WIDTH), BF16),
                        pltpu.VMEM((tm, GROUP_WIDTH), BF16)],
        compiler_params=pltpu.CompilerParams(
            dimension_semantics=("parallel", "parallel"),
            vmem_limit_bytes=VMEM_LIMIT_BYTES),
        name="merge_ffn2",
    )(h1, p, p, *o, *lse, mod, gmix, g3, gfin, wg, wpool, pscale, wpb, wab, wout, w2i, w2o)


def kernel(x, c, positions, w_ada, b_ada, g_norm_ffn1, w_ffn1_in, w_ffn1_out, g_norm_mix, w_in, w_pool,
           pool_scale, w_pool_branch, w_attn_branch, w_out, g_norm_ffn2, w_ffn2_in, w_ffn2_out, g_final):
    assert w_ada.shape[0] == 1, "single-layer block"
    batch, seq, d_model = x.shape
    assert d_model == D_MODEL and seq % TOKEN_TILE == 0
    assert all((seq // d) % QBLK == 0 and ATTN_ROWS % min(ATTN_ROWS, seq // d) == 0 for d in ATTN_DILATIONS)

    mod = _ada_modulation(c, w_ada[0], b_ada[0]).reshape(batch, N_MOD, D_MODEL)

    row = lambda g: g.reshape(1, -1)
    bf = lambda w: w.astype(BF16)
    freq = ROPE_THETA ** (-jnp.arange(0, HEAD_DIM, 2, dtype=F32) / HEAD_DIM)
    invf = jnp.tile(freq, LANES // (HEAD_DIM // 2)).reshape(1, LANES)
    pos4 = positions.reshape(batch, seq // TOKEN_TILE, TOKEN_TILE // LANES, LANES)

    col = jnp.arange(PQKV_WIDTH)
    q_cols = (col >= POOL_WIDTH) & (col < POOL_WIDTH + ATTN_WIDTH)
    w_in0 = w_in[0]
    w_pqkv = w_in0[:, :PQKV_WIDTH] * jnp.where(q_cols, HEAD_DIM ** -0.5, 1.0).astype(F32)
    h1, p, *kvq = _ffn1_proj(x, mod, row(g_norm_ffn1[0]), row(g_norm_mix[0]), pos4, invf,
                             bf(w_ffn1_in[0]), bf(w_ffn1_out[0]), bf(w_pqkv))

    o, lse = zip(*[_dilated_attention(t) for t in kvq])

    w_pool_bd = jnp.zeros((POOL_WIDTH, POOL_WIDTH), F32)
    for gi in range(len(POOL_WINDOWS)):
        sl = slice(gi * POOL_GROUP_DIM, (gi + 1) * POOL_GROUP_DIM)
        w_pool_bd = w_pool_bd.at[sl, sl].set(w_pool[0, gi])

    return _merge_ffn2(h1, p, o, lse, mod, row(g_norm_mix[0]), row(g_norm_ffn2[0]), row(g_final),
                       bf(w_in0[:, PQKV_WIDTH:]), bf(w_pool_bd), row(pool_scale[0]),
                       bf(w_pool_branch[0]), bf(w_attn_branch[0]), bf(w_out[0]),
                       bf(w_ffn2_in[0]), bf(w_ffn2_out[0]))
```

```python
import jax
import jax.numpy as jnp
from jax import lax
from jax.experimental import pallas as pl
from jax.experimental.pallas import tpu as pltpu

F32 = jnp.float32
BF16 = jnp.bfloat16

D_MODEL = 1024
POOL_WINDOWS = (2, 4, 8, 16)
POOL_GROUP_DIM = 64
POOL_WIDTH = 256
ATTN_DILATIONS = (1, 4, 16)
N_GROUPS = 3
HEAD_DIM = 64
GROUP_WIDTH = 256
ATTN_WIDTH = N_GROUPS * GROUP_WIDTH
PQKV_WIDTH = POOL_WIDTH + 3 * ATTN_WIDTH
KVQ_WIDTH = 3 * GROUP_WIDTH
D_FF = 2816
N_MOD = 9
ROPE_THETA = 10000.0
EPS = 1e-6
NEG_BIG = -1e30

LANES = 128
VMEM_LIMIT_BYTES = 60 * 1024 * 1024

TOKEN_TILE = 512
FF_CHUNKS = ((0, 1024), (1024, 2048), (2048, D_FF))
ATTN_ROWS = 4096
QBLK = 128
POOL_HALO = 16
SLABS = GROUP_WIDTH // LANES
ROW_PARTS = 2
GATE_CHUNKS = 2
FIRST_PARTS = 4
FINAL_PARTS = 4
STAGE_PITCH = 24
WIDE_PITCH = 40
VEC_ROWS = 64


def _resident(shape):
    nd = len(shape)
    return pl.BlockSpec(shape, lambda *_: (0,) * nd, pipeline_mode=pl.Buffered(1))


def _rms_modulate(x, g, shift, scale):
    ms = jnp.mean(x * x, axis=-1, keepdims=True)
    return x * lax.rsqrt(ms + EPS) * (g * (1.0 + scale)) + shift


def _dot(a, b):
    return jnp.dot(a, b, preferred_element_type=F32)


def _sigmoid(x):
    return 0.5 * jnp.tanh(0.5 * x) + 0.5


def _row_parts(x, parts):
    n = x.shape[0] // parts
    return [x[i * n:(i + 1) * n] for i in range(parts)]


def _dot_parts(parts, w):
    return jnp.concatenate([_dot(part, w) for part in parts], axis=0)


def _swiglu(u_parts, w_in_ref, w_out_ref, out_parts):
    u = jnp.concatenate(u_parts, axis=0)

    def up(idx):
        c0, c1 = FF_CHUNKS[idx]
        wa, wb = w_in_ref[:, c0:c1], w_in_ref[:, D_FF + c0:D_FF + c1]
        if idx == 0:
            return _dot_parts(u_parts, wa), _dot_parts(u_parts, wb)
        return _dot(u, wa), _dot(u, wb)

    y = None
    pending = up(0)
    last = len(FF_CHUNKS) - 1
    for idx, (c0, c1) in enumerate(FF_CHUNKS):
        following = up(idx + 1) if idx < last else None
        a, b = pending
        act = (a * _sigmoid(a) * b).astype(BF16)
        if idx == last:
            return [yp + _dot(ap, w_out_ref[c0:c1, :])
                    for yp, ap in zip(_row_parts(y, out_parts), _row_parts(act, out_parts))]
        part = _dot(act, w_out_ref[c0:c1, :])
        y = part if y is None else y + part
        pending = following


def _ada_kernel(c_ref, w_ref, b_ref, o_ref):
    c = c_ref[...]
    cond = c * jax.nn.sigmoid(c)
    c_hi = cond.astype(BF16)
    c_lo = (cond - c_hi.astype(F32)).astype(BF16)
    w = w_ref[...]
    w_hi = w.astype(BF16)
    w_lo = (w - w_hi.astype(F32)).astype(BF16)
    acc = jnp.dot(c_hi, w_hi, preferred_element_type=F32)
    acc += jnp.dot(c_hi, w_lo, preferred_element_type=F32)
    acc += jnp.dot(c_lo, w_hi, preferred_element_type=F32)
    o_ref[...] = acc + b_ref[...]


def _ada_modulation(c, w_ada, b_ada):
    batch = c.shape[0]
    return pl.pallas_call(
        _ada_kernel,
        grid=(N_MOD,),
        in_specs=[
            pl.BlockSpec((batch, D_MODEL), lambda j: (0, 0)),
            pl.BlockSpec((D_MODEL, D_MODEL), lambda j: (0, j)),
            pl.BlockSpec((1, D_MODEL), lambda j: (0, j)),
        ],
        out_specs=pl.BlockSpec((batch, D_MODEL), lambda j: (0, j)),
        out_shape=jax.ShapeDtypeStruct((batch, N_MOD * D_MODEL), F32),
        compiler_params=pltpu.CompilerParams(dimension_semantics=("parallel",)),
        name="ada_modulation",
    )(c, w_ada, b_ada.reshape(1, N_MOD * D_MODEL))


def _ffn1_proj_kernel(x_ref, mod_ref, g1_ref, g2_ref, pos_ref, invf_ref,
                      w1i_ref, w1o_ref, wp_ref,
                      h_ref, p_ref, kvq0_ref, kvq1_ref, kvq2_ref, trig_ref, stage_ref, wide_ref):
    tm = x_ref.shape[1]
    n_chunks = tm // QBLK
    half = HEAD_DIM // 2
    assert n_chunks * half == LANES

    lane = lax.broadcasted_iota(jnp.int32, (QBLK, LANES), 1)
    quarter = lane >> 5
    first_half = (lane & (HEAD_DIM - 1)) < half
    pos_c = None
    for j in range(n_chunks):
        pos_row = pos_ref[0, 0, j:j + 1, :].astype(F32)
        pos_col = jnp.broadcast_to(pos_row, (QBLK, LANES)).T
        pos_c = pos_col if j == 0 else jnp.where(quarter == j, pos_col, pos_c)
    ang = pos_c * invf_ref[...]
    for t, tbl in enumerate((jnp.cos(ang), jnp.sin(ang))):
        rolled = [tbl] + [pltpu.roll(tbl, half * k, 1) for k in range(1, n_chunks)]
        for j in range(n_chunks):
            full = rolled[(-j) % n_chunks]
            for ql in range(1, n_chunks):
                full = jnp.where(quarter == ql, rolled[(ql - j) % n_chunks], full)
            if t == 1:
                full = jnp.where(first_half, -full, full)
            trig_ref[t, j * QBLK:(j + 1) * QBLK, :] = full

    x = x_ref[0]
    mod = mod_ref[0]
    u_parts = [_rms_modulate(xp, g1_ref[...], mod[0:1], mod[1:2]).astype(BF16) for xp in _row_parts(x, FIRST_PARTS)]
    h = x + (0.5 * mod[2:3]) * _swiglu(u_parts, w1i_ref, w1o_ref, 1)[0]
    h_ref[0] = h
    u2 = _rms_modulate(h, g2_ref[...], mod[3:4], mod[4:5]).astype(BF16)

    qbase, kbase, vbase = POOL_WIDTH, POOL_WIDTH + ATTN_WIDTH, POOL_WIDTH + 2 * ATTN_WIDTH

    def emit(val, which, g, sl):
        if g == 0:
            kvq0_ref[0, 0, :, which * GROUP_WIDTH + sl * LANES:which * GROUP_WIDTH + (sl + 1) * LANES] = val.astype(BF16)
        elif g == 1:
            stage_ref[which * SLABS + sl] = val
        else:
            d = ATTN_DILATIONS[g]
            for m in range(tm // d):
                wide_ref[which * SLABS + sl, m * STAGE_PITCH:m * STAGE_PITCH + d, :] = val[m * d:(m + 1) * d]

    def rotary(t):
        rot = jnp.where(jnp.tile(first_half, (n_chunks, 1)), pltpu.roll(t, LANES - half, 1), pltpu.roll(t, half, 1))
        return t * trig_ref[0] + rot * trig_ref[1]

    pk = _dot(u2, wp_ref[:, kbase:kbase + ATTN_WIDTH])
    pq = _dot(u2, wp_ref[:, qbase:qbase + ATTN_WIDTH])
    for s in range(N_GROUPS * SLABS):
        emit(rotary(pk[:, s * LANES:(s + 1) * LANES]), 0, s // SLABS, s % SLABS)
    pv = _dot(u2, wp_ref[:, vbase:vbase + ATTN_WIDTH])
    for s in range(N_GROUPS * SLABS):
        emit(rotary(pq[:, s * LANES:(s + 1) * LANES]), 2, s // SLABS, s % SLABS)
    pp = _dot(u2, wp_ref[:, :POOL_WIDTH])
    for s in range(N_GROUPS * SLABS):
        emit(pv[:, s * LANES:(s + 1) * LANES], 1, s // SLABS, s % SLABS)

    d1, d2 = ATTN_DILATIONS[1], ATTN_DILATIONS[2]
    for which in range(3):
        for sl in range(SLABS):
            cols = slice(which * GROUP_WIDTH + sl * LANES, which * GROUP_WIDTH + (sl + 1) * LANES)
            for r in range(d1):
                kvq1_ref[0, r, :, cols] = stage_ref[which * SLABS + sl, pl.ds(r, tm // d1, stride=d1), :].astype(BF16)
            for r in range(d2):
                kvq2_ref[0, r, :, cols] = \
                    wide_ref[which * SLABS + sl, pl.ds(r, tm // d2, stride=STAGE_PITCH), :].astype(BF16)
    p_ref[0] = pp


def _ffn1_proj(x, mod, g1, g2, pos4, invf, w1i, w1o, wp):
    batch, seq, _ = x.shape
    tm = TOKEN_TILE
    tile = lambda w: pl.BlockSpec((1, tm, w), lambda b, i: (b, i, 0))
    kvq_specs = [pl.BlockSpec((1, d, tm // d, KVQ_WIDTH), lambda b, i: (b, 0, i, 0)) for d in ATTN_DILATIONS]
    kvq_shapes = [jax.ShapeDtypeStruct((batch, d, seq // d, KVQ_WIDTH), BF16) for d in ATTN_DILATIONS]
    return pl.pallas_call(
        _ffn1_proj_kernel,
        grid=(batch, seq // tm),
        in_specs=[
            tile(D_MODEL),
            pl.BlockSpec((1, N_MOD, D_MODEL), lambda b, i: (b, 0, 0)),
            _resident((1, D_MODEL)),
            _resident((1, D_MODEL)),
            pl.BlockSpec((1, 1, tm // LANES, LANES), lambda b, i: (b, i, 0, 0)),
            _resident((1, LANES)),
            _resident(w1i.shape),
            _resident(w1o.shape),
            _resident(wp.shape),
        ],
        out_specs=[tile(D_MODEL), tile(POOL_WIDTH)] + kvq_specs,
        out_shape=[
            jax.ShapeDtypeStruct((batch, seq, D_MODEL), F32),
            jax.ShapeDtypeStruct((batch, seq, POOL_WIDTH), F32),
        ] + kvq_shapes,
        scratch_shapes=[pltpu.VMEM((2, tm, LANES), F32),
                        pltpu.VMEM((3 * SLABS, tm, LANES), F32),
                        pltpu.VMEM((3 * SLABS, tm // ATTN_DILATIONS[2] * STAGE_PITCH, LANES), F32)],
        compiler_params=pltpu.CompilerParams(
            dimension_semantics=("parallel", "parallel"),
            vmem_limit_bytes=VMEM_LIMIT_BYTES),
        name="ffn1_proj",
    )(x, mod, g1, g2, pos4, invf, w1i, w1o, wp)


def _attn_kernel(cur_ref, prev_ref, o_ref, lse_ref):
    n_res = cur_ref.shape[1]
    nq = cur_ref.shape[2] // QBLK
    has_prev = pl.program_id(2) > 0

    a = lax.broadcasted_iota(jnp.int32, (2 * QBLK, 2 * QBLK), 0) & (QBLK - 1)
    c = lax.broadcasted_iota(jnp.int32, (2 * QBLK, 2 * QBLK), 1)
    bias = jnp.where((c >= a) & (c <= a + QBLK), 0.0, NEG_BIG).astype(F32)
    bias_first = jnp.where((c >= QBLK) | has_prev, bias, NEG_BIG)
    head0 = lax.broadcasted_iota(jnp.int32, (QBLK, LANES), 1) < HEAD_DIM
    ones = jnp.ones((2 * QBLK, LANES), BF16)
    dn = (((1,), (1,)), ((), ()))

    def band(r, j, cols):
        if j == 0:
            return jnp.concatenate([prev_ref[0, r, :, cols], cur_ref[0, r, :QBLK, cols]], axis=0)
        return cur_ref[0, r, (j - 1) * QBLK:(j + 1) * QBLK, cols]

    blocks = [(r, sl, j) for r in range(n_res) for sl in range(SLABS) for j in range(nq)]
    scores = []
    for r, sl, j in blocks:
        qs = cur_ref[0, r, j * QBLK:(j + 1) * QBLK, 2 * GROUP_WIDTH + sl * LANES:2 * GROUP_WIDTH + (sl + 1) * LANES]
        zero = jnp.zeros_like(qs)
        lhs = jnp.concatenate([jnp.where(head0, qs, zero), jnp.where(head0, zero, qs)], axis=0)
        s = lax.dot_general(lhs, band(r, j, slice(sl * LANES, (sl + 1) * LANES)), dn, preferred_element_type=F32)
        scores.append(s + (bias_first if j == 0 else bias))
    probs = []
    for s in scores:
        m = jnp.max(s, axis=-1, keepdims=True)
        probs.append((jnp.exp(s - m).astype(BF16), m))
    for (r, sl, j), (p, m) in zip(blocks, probs):
        vband = band(r, j, slice(GROUP_WIDTH + sl * LANES, GROUP_WIDTH + (sl + 1) * LANES))
        out = jnp.dot(p, jnp.concatenate([vband, ones], axis=1), preferred_element_type=F32)
        num = jnp.where(head0, out[:QBLK, :LANES], out[QBLK:, :LANES])
        den = jnp.where(head0, out[:QBLK, LANES:], out[QBLK:, LANES:])
        mm = jnp.where(head0, jnp.broadcast_to(m[:QBLK], (QBLK, LANES)), jnp.broadcast_to(m[QBLK:], (QBLK, LANES)))
        rows = slice(j * QBLK, (j + 1) * QBLK)
        ocols = slice(sl * LANES, (sl + 1) * LANES)
        o_ref[0, r, rows, ocols] = num / den
        lse_ref[0, r, rows, ocols] = mm + jnp.log(den)


def _dilated_attention(kvq):
    batch, d, length, _ = kvq.shape
    n = min(ATTN_ROWS, length)
    n_res = ATTN_ROWS // n
    cur = pl.BlockSpec((1, n_res, n, KVQ_WIDTH), lambda b, r, i: (b, r, i, 0))
    prev = pl.BlockSpec((1, n_res, QBLK, 2 * GROUP_WIDTH),
                        lambda b, r, i: (b, r, jnp.maximum(i * (n // QBLK) - 1, 0), 0))
    out = pl.BlockSpec((1, n_res, n, GROUP_WIDTH), lambda b, r, i: (b, r, i, 0))
    out_shape = jax.ShapeDtypeStruct((batch, d, length, GROUP_WIDTH), F32)
    return pl.pallas_call(
        _attn_kernel,
        grid=(batch, d // n_res, length // n),
        in_specs=[cur, prev],
        out_specs=[out, out],
        out_shape=[out_shape, out_shape],
        compiler_params=pltpu.CompilerParams(
            dimension_semantics=("parallel", "parallel", "parallel"),
            vmem_limit_bytes=VMEM_LIMIT_BYTES),
        name=f"dilated_attn_d{d}",
    )(kvq, kvq)


def _merge_ffn2_kernel(h_ref, p_ref, pprev_ref, o0_ref, o1_ref, o2_ref, l0_ref, l1_ref, l2_ref,
                       mod_ref, gmix_ref, g3_ref, gfin_ref,
                       wg_ref, wpool_ref, pscale_ref, wpb_ref, wab_ref, wout_ref, w2i_ref, w2o_ref,
                       out_ref, tok_ref, wide_ref, xs_ref, dpool_ref, yattn_ref):
    tm = h_ref.shape[1]
    i = pl.program_id(1)
    mod = mod_ref[0]
    h_parts = _row_parts(h_ref[0], ROW_PARTS)
    u_parts = [_rms_modulate(hp, gmix_ref[...], mod[3:4], mod[4:5]).astype(BF16) for hp in h_parts]
    u = jnp.concatenate(u_parts, axis=0)
    z_pool = _dot_parts(u_parts, wg_ref[:, :D_MODEL])

    xs_ref[:POOL_HALO] = jnp.where(i > 0, pprev_ref[0], 0.0)
    xs_ref[POOL_HALO:] = p_ref[0]
    lane = lax.broadcasted_iota(jnp.int32, (VEC_ROWS, POOL_WIDTH), 1)
    row = lax.broadcasted_iota(jnp.int32, (VEC_ROWS, POOL_WIDTH), 0)
    for r0 in range(0, tm, VEC_ROWS):
        sums = {1: xs_ref[r0:r0 + VEC_ROWS + POOL_HALO]}
        w = 1
        while w < POOL_WINDOWS[-1]:
            t = sums[w]
            sums[2 * w] = t[w:] + t[:-w]
            w *= 2
        trail = jnp.zeros((VEC_ROWS, POOL_WIDTH), F32)
        win = jnp.zeros((VEC_ROWS, POOL_WIDTH), jnp.int32)
        for gi, wnd in enumerate(POOL_WINDOWS):
            in_group = (lane >> 6) == gi
            start = POOL_HALO - (wnd - 1)
            trail = jnp.where(in_group, sums[wnd][start:start + VEC_ROWS], trail)
            win = jnp.where(in_group, wnd, win)
        count = jnp.minimum(row + (i * tm + r0 + 1), win).astype(F32)
        dpool_ref[r0:r0 + VEC_ROWS] = (trail / count - sums[1][POOL_HALO:]).astype(BF16)
    y_pool = (_dot(dpool_ref[...], wpool_ref[...]) * pscale_ref[...]).astype(BF16)
    b_pool = _dot(y_pool, wpb_ref[...])

    gw = D_MODEL // GATE_CHUNKS
    z_attn = [_dot(u, wg_ref[:, D_MODEL + c * gw:D_MODEL + (c + 1) * gw]) for c in range(GATE_CHUNKS)]

    d1, d2 = ATTN_DILATIONS[1], ATTN_DILATIONS[2]
    for a, ref in enumerate((o1_ref, l1_ref)):
        for sl in range(SLABS):
            for r in range(d1):
                tok_ref[a * SLABS + sl, pl.ds(r, tm // d1, stride=d1), :] = ref[0, r, :, sl * LANES:(sl + 1) * LANES]
    for a, ref in enumerate((o2_ref, l2_ref)):
        for sl in range(SLABS):
            for r in range(d2):
                wide_ref[a * SLABS + sl, r * WIDE_PITCH:r * WIDE_PITCH + tm // d2, :] = \
                    ref[0, r, :, sl * LANES:(sl + 1) * LANES]

    def wide_rows(a, sl, r0):
        return jnp.concatenate([wide_ref[a * SLABS + sl, pl.ds(m, d2, stride=WIDE_PITCH), :]
                                for m in range(r0 // d2, (r0 + VEC_ROWS) // d2)], axis=0)

    for sl in range(SLABS):
        cols = slice(sl * LANES, (sl + 1) * LANES)
        for r0 in range(0, tm, VEC_ROWS):
            rows = slice(r0, r0 + VEC_ROWS)
            o0, l0 = o0_ref[0, 0, rows, cols], l0_ref[0, 0, rows, cols]
            o1, l1 = tok_ref[0 * SLABS + sl, rows, :], tok_ref[1 * SLABS + sl, rows, :]
            o2, l2 = wide_rows(0, sl, r0), wide_rows(1, sl, r0)
            mx = jnp.maximum(jnp.maximum(l0, l1), l2)
            e0, e1, e2 = jnp.exp(l0 - mx), jnp.exp(l1 - mx), jnp.exp(l2 - mx)
            yattn_ref[rows, cols] = ((e0 * o0 + e1 * o1 + e2 * o2) / (e0 + e1 + e2)).astype(BF16)
    y_attn = yattn_ref[...]
    b_attn = _dot(y_attn, wab_ref[...])

    gated_pool = _sigmoid(z_pool) * b_pool
    mix = None
    for c in range(GATE_CHUNKS):
        cols = slice(c * gw, (c + 1) * gw)
        merged = (gated_pool[:, cols] + _sigmoid(z_attn[c]) * b_attn[:, cols]).astype(BF16)
        w_rows = wout_ref[cols, :]
        if c + 1 < GATE_CHUNKS:
            part = _dot(merged, w_rows)
            mix = part if mix is None else mix + part
        else:
            mix_parts = [mp + _dot(gp, w_rows)
                         for mp, gp in zip(_row_parts(mix, ROW_PARTS), _row_parts(merged, ROW_PARTS))]
    h_parts = [hp + mod[5:6] * mp for hp, mp in zip(h_parts, mix_parts)]

    u3_parts = [_rms_modulate(hp, g3_ref[...], mod[6:7], mod[7:8]).astype(BF16) for hp in h_parts]
    y_parts = _swiglu(u3_parts, w2i_ref, w2o_ref, FINAL_PARTS)
    n = tm // FINAL_PARTS
    hf_parts = _row_parts(jnp.concatenate(h_parts, axis=0), FINAL_PARTS)
    for part, (hp, yp) in enumerate(zip(hf_parts, y_parts)):
        hf = hp + (0.5 * mod[8:9]) * yp
        ms = jnp.mean(hf * hf, axis=-1, keepdims=True)
        out_ref[0, part * n:(part + 1) * n, :] = hf * lax.rsqrt(ms + EPS) * gfin_ref[...]


def _merge_ffn2(h1, p, o, lse, mod, gmix, g3, gfin, wg, wpool, pscale, wpb, wab, wout, w2i, w2o):
    batch, seq, _ = h1.shape
    tm = TOKEN_TILE
    tile = lambda w: pl.BlockSpec((1, tm, w), lambda b, i: (b, i, 0))
    halo = pl.BlockSpec((1, POOL_HALO, POOL_WIDTH),
                        lambda b, i: (b, jnp.maximum(i * (tm // POOL_HALO) - 1, 0), 0))
    grp = [pl.BlockSpec((1, d, tm // d, GROUP_WIDTH), lambda b, i: (b, 0, i, 0)) for d in ATTN_DILATIONS]
    vec = _resident((1, D_MODEL))
    return pl.pallas_call(
        _merge_ffn2_kernel,
        grid=(batch, seq // tm),
        in_specs=[
            tile(D_MODEL), tile(POOL_WIDTH), halo,
            *grp, *grp,
            pl.BlockSpec((1, N_MOD, D_MODEL), lambda b, i: (b, 0, 0)),
            vec, vec, vec,
            _resident(wg.shape), _resident(wpool.shape), _resident(pscale.shape),
            _resident(wpb.shape), _resident(wab.shape), _resident(wout.shape),
            _resident(w2i.shape), _resident(w2o.shape),
        ],
        out_specs=tile(D_MODEL),
        out_shape=jax.ShapeDtypeStruct((batch, seq, D_MODEL), F32),
        scratch_shapes=[pltpu.VMEM((2 * SLABS, tm, LANES), F32),
                        pltpu.VMEM((2 * SLABS, ATTN_DILATIONS[2] * WIDE_PITCH, LANES), F32),
                        pltpu.VMEM((tm + POOL_HALO, POOL_WIDTH), F32),
                        pltpu.VMEM((tm, POOL_WIDTH), BF16),
                        pltpu.VMEM((tm, GROUP_WIDTH), BF16)],
        compiler_params=pltpu.CompilerParams(
            dimension_semantics=("parallel", "parallel"),
            vmem_limit_bytes=VMEM_LIMIT_BYTES),
        name="merge_ffn2",
    )(h1, p, p, *o, *lse, mod, gmix, g3, gfin, wg, wpool, pscale, wpb, wab, wout, w2i, w2o)


def kernel(x, c, positions, w_ada, b_ada, g_norm_ffn1, w_ffn1_in, w_ffn1_out, g_norm_mix, w_in, w_pool,
           pool_scale, w_pool_branch, w_attn_branch, w_out, g_norm_ffn2, w_ffn2_in, w_ffn2_out, g_final):
    assert w_ada.shape[0] == 1, "single-layer block"
    batch, seq, d_model = x.shape
    assert d_model == D_MODEL and seq % TOKEN_TILE == 0
    assert all((seq // d) % QBLK == 0 and ATTN_ROWS % min(ATTN_ROWS, seq // d) == 0 for d in ATTN_DILATIONS)

    mod = _ada_modulation(c, w_ada[0], b_ada[0]).reshape(batch, N_MOD, D_MODEL)

    row = lambda g: g.reshape(1, -1)
    bf = lambda w: w.astype(BF16)
    freq = ROPE_THETA ** (-jnp.arange(0, HEAD_DIM, 2, dtype=F32) / HEAD_DIM)
    invf = jnp.tile(freq, LANES // (HEAD_DIM // 2)).reshape(1, LANES)
    pos4 = positions.reshape(batch, seq // TOKEN_TILE, TOKEN_TILE // LANES, LANES)

    col = jnp.arange(PQKV_WIDTH)
    q_cols = (col >= POOL_WIDTH) & (col < POOL_WIDTH + ATTN_WIDTH)
    w_in0 = w_in[0]
    w_pqkv = w_in0[:, :PQKV_WIDTH] * jnp.where(q_cols, HEAD_DIM ** -0.5, 1.0).astype(F32)
    h1, p, *kvq = _ffn1_proj(x, mod, row(g_norm_ffn1[0]), row(g_norm_mix[0]), pos4, invf,
                             bf(w_ffn1_in[0]), bf(w_ffn1_out[0]), bf(w_pqkv))

    o, lse = zip(*[_dilated_attention(t) for t in kvq])

    w_pool_bd = jnp.zeros((POOL_WIDTH, POOL_WIDTH), F32)
    for gi in range(len(POOL_WINDOWS)):
        sl = slice(gi * POOL_GROUP_DIM, (gi + 1) * POOL_GROUP_DIM)
        w_pool_bd = w_pool_bd.at[sl, sl].set(w_pool[0, gi])

    return _merge_ffn2(h1, p, o, lse, mod, row(g_norm_mix[0]), row(g_norm_ffn2[0]), row(g_final),
                       bf(w_in0[:, PQKV_WIDTH:]), bf(w_pool_bd), row(pool_scale[0]),
                       bf(w_pool_branch[0]), bf(w_attn_branch[0]), bf(w_out[0]),
                       bf(w_ffn2_in[0]), bf(w_ffn2_out[0]))
```

```python
import jax
import jax.numpy as jnp
from jax import lax
from jax.experimental import pallas as pl
from jax.experimental.pallas import tpu as pltpu

F32 = jnp.float32
BF16 = jnp.bfloat16

D_MODEL = 1024
POOL_WINDOWS = (2, 4, 8, 16)
POOL_GROUP_DIM = 64
POOL_WIDTH = 256
ATTN_DILATIONS = (1, 4, 16)
N_GROUPS = 3
HEAD_DIM = 64
GROUP_WIDTH = 256
ATTN_WIDTH = N_GROUPS * GROUP_WIDTH
PQKV_WIDTH = POOL_WIDTH + 3 * ATTN_WIDTH
KVQ_WIDTH = 3 * GROUP_WIDTH
D_FF = 2816
N_MOD = 9
ROPE_THETA = 10000.0
EPS = 1e-6
NEG_BIG = -1e30

LANES = 128
VMEM_LIMIT_BYTES = 60 * 1024 * 1024

ADA_ROWS = 128
TOKEN_TILE = 512
FF_CHUNKS = ((0, 1024), (1024, 2048), (2048, D_FF))
ATTN_ROWS = 4096
QBLK = 128
POOL_HALO = 16
SLABS = GROUP_WIDTH // LANES
ROW_PARTS = 2
GATE_CHUNKS = 2
FIRST_PARTS = 4
FINAL_PARTS = 4
STAGE_PITCH = 24
WIDE_PITCH = 40
VEC_ROWS = 64


def _resident(shape):
    nd = len(shape)
    return pl.BlockSpec(shape, lambda *_: (0,) * nd, pipeline_mode=pl.Buffered(1))


def _rms_modulate(x, g, shift, scale):
    ms = jnp.mean(x * x, axis=-1, keepdims=True)
    return x * lax.rsqrt(ms + EPS) * (g * (1.0 + scale)) + shift


def _dot(a, b):
    return jnp.dot(a, b, preferred_element_type=F32)


def _sigmoid(x):
    return 0.5 * jnp.tanh(0.5 * x) + 0.5


def _row_parts(x, parts):
    n = x.shape[0] // parts
    return [x[i * n:(i + 1) * n] for i in range(parts)]


def _dot_parts(parts, w):
    return jnp.concatenate([_dot(part, w) for part in parts], axis=0)


def _swiglu(u_parts, w_in_ref, w_out_ref, out_parts):
    u = jnp.concatenate(u_parts, axis=0)

    def up(idx):
        c0, c1 = FF_CHUNKS[idx]
        wa, wb = w_in_ref[:, c0:c1], w_in_ref[:, D_FF + c0:D_FF + c1]
        if idx == 0:
            return _dot_parts(u_parts, wa), _dot_parts(u_parts, wb)
        return _dot(u, wa), _dot(u, wb)

    y = None
    pending = up(0)
    last = len(FF_CHUNKS) - 1
    for idx, (c0, c1) in enumerate(FF_CHUNKS):
        following = up(idx + 1) if idx < last else None
        a, b = pending
        act = (a * _sigmoid(a) * b).astype(BF16)
        if idx == last:
            return [yp + _dot(ap, w_out_ref[c0:c1, :])
                    for yp, ap in zip(_row_parts(y, out_parts), _row_parts(act, out_parts))]
        part = _dot(act, w_out_ref[c0:c1, :])
        y = part if y is None else y + part
        pending = following


def _ada_kernel(c_ref, w_ref, b_ref, o_ref):
    k = pl.program_id(0)
    c = c_ref[...]
    cond = c * jax.nn.sigmoid(c)
    c_hi = cond.astype(BF16)
    c_lo = (cond - c_hi.astype(F32)).astype(BF16)
    w = w_ref[...]
    w_hi = w.astype(BF16)
    w_lo = (w - w_hi.astype(F32)).astype(BF16)
    acc = jnp.dot(c_hi, w_hi, preferred_element_type=F32)
    acc += jnp.dot(c_hi, w_lo, preferred_element_type=F32)
    acc += jnp.dot(c_lo, w_hi, preferred_element_type=F32)

    @pl.when(k == 0)
    def _():
        o_ref[...] = acc + b_ref[...]

    @pl.when(k > 0)
    def _():
        o_ref[...] += acc


def _ada_modulation(c, w_ada, b_ada):
    batch = c.shape[0]
    width = N_MOD * D_MODEL
    return pl.pallas_call(
        _ada_kernel,
        grid=(D_MODEL // ADA_ROWS,),
        in_specs=[
            pl.BlockSpec((batch, ADA_ROWS), lambda k: (0, k)),
            pl.BlockSpec((ADA_ROWS, width), lambda k: (k, 0)),
            pl.BlockSpec((1, width), lambda k: (0, 0)),
        ],
        out_specs=pl.BlockSpec((batch, width), lambda k: (0, 0)),
        out_shape=jax.ShapeDtypeStruct((batch, width), F32),
        compiler_params=pltpu.CompilerParams(dimension_semantics=("arbitrary",)),
        name="ada_modulation",
    )(c, w_ada, b_ada.reshape(1, width))


def _ffn1_proj_kernel(x_ref, mod_ref, g1_ref, g2_ref, pos_ref, invf_ref,
                      w1i_ref, w1o_ref, wp_ref,
                      h_ref, p_ref, kvq0_ref, kvq1_ref, kvq2_ref, trig_ref, stage_ref, wide_ref):
    tm = x_ref.shape[1]
    n_chunks = tm // QBLK
    half = HEAD_DIM // 2
    assert n_chunks * half == LANES

    lane = lax.broadcasted_iota(jnp.int32, (QBLK, LANES), 1)
    quarter = lane >> 5
    first_half = (lane & (HEAD_DIM - 1)) < half
    pos_c = None
    for j in range(n_chunks):
        pos_row = pos_ref[0, 0, j:j + 1, :].astype(F32)
        pos_col = jnp.broadcast_to(pos_row, (QBLK, LANES)).T
        pos_c = pos_col if j == 0 else jnp.where(quarter == j, pos_col, pos_c)
    ang = pos_c * invf_ref[...]
    for t, tbl in enumerate((jnp.cos(ang), jnp.sin(ang))):
        rolled = [tbl] + [pltpu.roll(tbl, half * k, 1) for k in range(1, n_chunks)]
        for j in range(n_chunks):
            full = rolled[(-j) % n_chunks]
            for ql in range(1, n_chunks):
                full = jnp.where(quarter == ql, rolled[(ql - j) % n_chunks], full)
            if t == 1:
                full = jnp.where(first_half, -full, full)
            trig_ref[t, j * QBLK:(j + 1) * QBLK, :] = full

    x = x_ref[0]
    mod = mod_ref[0]
    u_parts = [_rms_modulate(xp, g1_ref[...], mod[0:1], mod[1:2]).astype(BF16) for xp in _row_parts(x, FIRST_PARTS)]
    h = x + (0.5 * mod[2:3]) * _swiglu(u_parts, w1i_ref, w1o_ref, 1)[0]
    h_ref[0] = h
    u2 = _rms_modulate(h, g2_ref[...], mod[3:4], mod[4:5]).astype(BF16)

    qbase, kbase, vbase = POOL_WIDTH, POOL_WIDTH + ATTN_WIDTH, POOL_WIDTH + 2 * ATTN_WIDTH

    def emit(val, which, g, sl):
        if g == 0:
            kvq0_ref[0, 0, :, which * GROUP_WIDTH + sl * LANES:which * GROUP_WIDTH + (sl + 1) * LANES] = val.astype(BF16)
        elif g == 1:
            stage_ref[which * SLABS + sl] = val
        else:
            d = ATTN_DILATIONS[g]
            for m in range(tm // d):
                wide_ref[which * SLABS + sl, m * STAGE_PITCH:m * STAGE_PITCH + d, :] = val[m * d:(m + 1) * d]

    def rotary(t):
        rot = jnp.where(jnp.tile(first_half, (n_chunks, 1)), pltpu.roll(t, LANES - half, 1), pltpu.roll(t, half, 1))
        return t * trig_ref[0] + rot * trig_ref[1]

    pk = _dot(u2, wp_ref[:, kbase:kbase + ATTN_WIDTH])
    pq = _dot(u2, wp_ref[:, qbase:qbase + ATTN_WIDTH])
    for s in range(N_GROUPS * SLABS):
        emit(rotary(pk[:, s * LANES:(s + 1) * LANES]), 0, s // SLABS, s % SLABS)
    pv = _dot(u2, wp_ref[:, vbase:vbase + ATTN_WIDTH])
    for s in range(N_GROUPS * SLABS):
        emit(rotary(pq[:, s * LANES:(s + 1) * LANES]), 2, s // SLABS, s % SLABS)
    pp = _dot(u2, wp_ref[:, :POOL_WIDTH])
    for s in range(N_GROUPS * SLABS):
        emit(pv[:, s * LANES:(s + 1) * LANES], 1, s // SLABS, s % SLABS)

    d1, d2 = ATTN_DILATIONS[1], ATTN_DILATIONS[2]
    for which in range(3):
        for sl in range(SLABS):
            cols = slice(which * GROUP_WIDTH + sl * LANES, which * GROUP_WIDTH + (sl + 1) * LANES)
            for r in range(d1):
                kvq1_ref[0, r, :, cols] = stage_ref[which * SLABS + sl, pl.ds(r, tm // d1, stride=d1), :].astype(BF16)
            for r in range(d2):
                kvq2_ref[0, r, :, cols] = \
                    wide_ref[which * SLABS + sl, pl.ds(r, tm // d2, stride=STAGE_PITCH), :].astype(BF16)
    p_ref[0] = pp


def _ffn1_proj(x, mod, g1, g2, pos4, invf, w1i, w1o, wp):
    batch, seq, _ = x.shape
    tm = TOKEN_TILE
    tile = lambda w: pl.BlockSpec((1, tm, w), lambda b, i: (b, i, 0))
    kvq_specs = [pl.BlockSpec((1, d, tm // d, KVQ_WIDTH), lambda b, i: (b, 0, i, 0)) for d in ATTN_DILATIONS]
    kvq_shapes = [jax.ShapeDtypeStruct((batch, d, seq // d, KVQ_WIDTH), BF16) for d in ATTN_DILATIONS]
    return pl.pallas_call(
        _ffn1_proj_kernel,
        grid=(batch, seq // tm),
        in_specs=[
            tile(D_MODEL),
            pl.BlockSpec((1, N_MOD, D_MODEL), lambda b, i: (b, 0, 0)),
            _resident((1, D_MODEL)),
            _resident((1, D_MODEL)),
            pl.BlockSpec((1, 1, tm // LANES, LANES), lambda b, i: (b, i, 0, 0)),
            _resident((1, LANES)),
            _resident(w1i.shape),
            _resident(w1o.shape),
            _resident(wp.shape),
        ],
        out_specs=[tile(D_MODEL), tile(POOL_WIDTH)] + kvq_specs,
        out_shape=[
            jax.ShapeDtypeStruct((batch, seq, D_MODEL), F32),
            jax.ShapeDtypeStruct((batch, seq, POOL_WIDTH), F32),
        ] + kvq_shapes,
        scratch_shapes=[pltpu.VMEM((2, tm, LANES), F32),
                        pltpu.VMEM((3 * SLABS, tm, LANES), F32),
                        pltpu.VMEM((3 * SLABS, tm // ATTN_DILATIONS[2] * STAGE_PITCH, LANES), F32)],
        compiler_params=pltpu.CompilerParams(
            dimension_semantics=("parallel", "parallel"),
            vmem_limit_bytes=VMEM_LIMIT_BYTES),
        name="ffn1_proj",
    )(x, mod, g1, g2, pos4, invf, w1i, w1o, wp)


def _attn_kernel(cur_ref, prev_ref, o_ref, lse_ref):
    n_res = cur_ref.shape[1]
    nq = cur_ref.shape[2] // QBLK
    has_prev = pl.program_id(2) > 0

    a = lax.broadcasted_iota(jnp.int32, (2 * QBLK, 2 * QBLK), 0) & (QBLK - 1)
    c = lax.broadcasted_iota(jnp.int32, (2 * QBLK, 2 * QBLK), 1)
    bias = jnp.where((c >= a) & (c <= a + QBLK), 0.0, NEG_BIG).astype(F32)
    bias_first = jnp.where((c >= QBLK) | has_prev, bias, NEG_BIG)
    head0 = lax.broadcasted_iota(jnp.int32, (QBLK, LANES), 1) < HEAD_DIM
    ones = jnp.ones((2 * QBLK, LANES), BF16)
    dn = (((1,), (1,)), ((), ()))

    def band(r, j, cols):
        if j == 0:
            return jnp.concatenate([prev_ref[0, r, :, cols], cur_ref[0, r, :QBLK, cols]], axis=0)
        return cur_ref[0, r, (j - 1) * QBLK:(j + 1) * QBLK, cols]

    blocks = [(r, sl, j) for r in range(n_res) for sl in range(SLABS) for j in range(nq)]
    scores = []
    for r, sl, j in blocks:
        qs = cur_ref[0, r, j * QBLK:(j + 1) * QBLK, 2 * GROUP_WIDTH + sl * LANES:2 * GROUP_WIDTH + (sl + 1) * LANES]
        zero = jnp.zeros_like(qs)
        lhs = jnp.concatenate([jnp.where(head0, qs, zero), jnp.where(head0, zero, qs)], axis=0)
        s = lax.dot_general(lhs, band(r, j, slice(sl * LANES, (sl + 1) * LANES)), dn, preferred_element_type=F32)
        scores.append(s + (bias_first if j == 0 else bias))
    probs = []
    for s in scores:
        m = jnp.max(s, axis=-1, keepdims=True)
        probs.append((jnp.exp(s - m).astype(BF16), m))
    for (r, sl, j), (p, m) in zip(blocks, probs):
        vband = band(r, j, slice(GROUP_WIDTH + sl * LANES, GROUP_WIDTH + (sl + 1) * LANES))
        out = jnp.dot(p, jnp.concatenate([vband, ones], axis=1), preferred_element_type=F32)
        num = jnp.where(head0, out[:QBLK, :LANES], out[QBLK:, :LANES])
        den = jnp.where(head0, out[:QBLK, LANES:], out[QBLK:, LANES:])
        mm = jnp.where(head0, jnp.broadcast_to(m[:QBLK], (QBLK, LANES)), jnp.broadcast_to(m[QBLK:], (QBLK, LANES)))
        rows = slice(j * QBLK, (j + 1) * QBLK)
        ocols = slice(sl * LANES, (sl + 1) * LANES)
        o_ref[0, r, rows, ocols] = num / den
        lse_ref[0, r, rows, ocols] = mm + jnp.log(den)


def _dilated_attention(kvq):
    batch, d, length, _ = kvq.shape
    n = min(ATTN_ROWS, length)
    n_res = ATTN_ROWS // n
    cur = pl.BlockSpec((1, n_res, n, KVQ_WIDTH), lambda b, r, i: (b, r, i, 0))
    prev = pl.BlockSpec((1, n_res, QBLK, 2 * GROUP_WIDTH),
                        lambda b, r, i: (b, r, jnp.maximum(i * (n // QBLK) - 1, 0), 0))
    out = pl.BlockSpec((1, n_res, n, GROUP_WIDTH), lambda b, r, i: (b, r, i, 0))
    out_shape = jax.ShapeDtypeStruct((batch, d, length, GROUP_WIDTH), F32)
    return pl.pallas_call(
        _attn_kernel,
        grid=(batch, d // n_res, length // n),
        in_specs=[cur, prev],
        out_specs=[out, out],
        out_shape=[out_shape, out_shape],
        compiler_params=pltpu.CompilerParams(
            dimension_semantics=("parallel", "parallel", "parallel"),
            vmem_limit_bytes=VMEM_LIMIT_BYTES),
        name=f"dilated_attn_d{d}",
    )(kvq, kvq)


def _merge_ffn2_kernel(h_ref, p_ref, pprev_ref, o0_ref, o1_ref, o2_ref, l0_ref, l1_ref, l2_ref,
                       mod_ref, gmix_ref, g3_ref, gfin_ref,
                       wg_ref, wpool_ref, pscale_ref, wpb_ref, wab_ref, wout_ref, w2i_ref, w2o_ref,
                       out_ref, tok_ref, wide_ref, xs_ref, dpool_ref, yattn_ref):
    tm = h_ref.shape[1]
    i = pl.program_id(1)
    mod = mod_ref[0]
    h_parts = _row_parts(h_ref[0], ROW_PARTS)
    u_parts = [_rms_modulate(hp, gmix_ref[...], mod[3:4], mod[4:5]).astype(BF16) for hp in h_parts]
    u = jnp.concatenate(u_parts, axis=0)
    z_pool = _dot_parts(u_parts, wg_ref[:, PQKV_WIDTH:PQKV_WIDTH + D_MODEL])

    xs_ref[:POOL_HALO] = jnp.where(i > 0, pprev_ref[0], 0.0)
    xs_ref[POOL_HALO:] = p_ref[0]
    lane = lax.broadcasted_iota(jnp.int32, (VEC_ROWS, POOL_WIDTH), 1)
    row = lax.broadcasted_iota(jnp.int32, (VEC_ROWS, POOL_WIDTH), 0)
    for r0 in range(0, tm, VEC_ROWS):
        sums = {1: xs_ref[r0:r0 + VEC_ROWS + POOL_HALO]}
        w = 1
        while w < POOL_WINDOWS[-1]:
            t = sums[w]
            sums[2 * w] = t[w:] + t[:-w]
            w *= 2
        trail = jnp.zeros((VEC_ROWS, POOL_WIDTH), F32)
        win = jnp.zeros((VEC_ROWS, POOL_WIDTH), jnp.int32)
        for gi, wnd in enumerate(POOL_WINDOWS):
            in_group = (lane >> 6) == gi
            start = POOL_HALO - (wnd - 1)
            trail = jnp.where(in_group, sums[wnd][start:start + VEC_ROWS], trail)
            win = jnp.where(in_group, wnd, win)
        count = jnp.minimum(row + (i * tm + r0 + 1), win).astype(F32)
        dpool_ref[r0:r0 + VEC_ROWS] = (trail / count - sums[1][POOL_HALO:]).astype(BF16)
    y_pool = (_dot(dpool_ref[...], wpool_ref[...]) * pscale_ref[...]).astype(BF16)
    b_pool = _dot(y_pool, wpb_ref[...])

    gw = D_MODEL // GATE_CHUNKS
    gbase = PQKV_WIDTH + D_MODEL
    z_attn = [_dot(u, wg_ref[:, gbase + c * gw:gbase + (c + 1) * gw]) for c in range(GATE_CHUNKS)]

    d1, d2 = ATTN_DILATIONS[1], ATTN_DILATIONS[2]
    for a, ref in enumerate((o1_ref, l1_ref)):
        for sl in range(SLABS):
            for r in range(d1):
                tok_ref[a * SLABS + sl, pl.ds(r, tm // d1, stride=d1), :] = ref[0, r, :, sl * LANES:(sl + 1) * LANES]
    for a, ref in enumerate((o2_ref, l2_ref)):
        for sl in range(SLABS):
            for r in range(d2):
                wide_ref[a * SLABS + sl, r * WIDE_PITCH:r * WIDE_PITCH + tm // d2, :] = \
                    ref[0, r, :, sl * LANES:(sl + 1) * LANES]

    def wide_rows(a, sl, r0):
        return jnp.concatenate([wide_ref[a * SLABS + sl, pl.ds(m, d2, stride=WIDE_PITCH), :]
                                for m in range(r0 // d2, (r0 + VEC_ROWS) // d2)], axis=0)

    for sl in range(SLABS):
        cols = slice(sl * LANES, (sl + 1) * LANES)
        for r0 in range(0, tm, VEC_ROWS):
            rows = slice(r0, r0 + VEC_ROWS)
            o0, l0 = o0_ref[0, 0, rows, cols], l0_ref[0, 0, rows, cols]
            o1, l1 = tok_ref[0 * SLABS + sl, rows, :], tok_ref[1 * SLABS + sl, rows, :]
            o2, l2 = wide_rows(0, sl, r0), wide_rows(1, sl, r0)
            mx = jnp.maximum(jnp.maximum(l0, l1), l2)
            e0, e1, e2 = jnp.exp(l0 - mx), jnp.exp(l1 - mx), jnp.exp(l2 - mx)
            yattn_ref[rows, cols] = ((e0 * o0 + e1 * o1 + e2 * o2) / (e0 + e1 + e2)).astype(BF16)
    y_attn = yattn_ref[...]
    b_attn = _dot(y_attn, wab_ref[...])

    gated_pool = _sigmoid(z_pool) * b_pool
    mix = None
    for c in range(GATE_CHUNKS):
        cols = slice(c * gw, (c + 1) * gw)
        merged = (gated_pool[:, cols] + _sigmoid(z_attn[c]) * b_attn[:, cols]).astype(BF16)
        w_rows = wout_ref[cols, :]
        if c + 1 < GATE_CHUNKS:
            part = _dot(merged, w_rows)
            mix = part if mix is None else mix + part
        else:
            mix_parts = [mp + _dot(gp, w_rows)
                         for mp, gp in zip(_row_parts(mix, ROW_PARTS), _row_parts(merged, ROW_PARTS))]
    h_parts = [hp + mod[5:6] * mp for hp, mp in zip(h_parts, mix_parts)]

    u3_parts = [_rms_modulate(hp, g3_ref[...], mod[6:7], mod[7:8]).astype(BF16) for hp in h_parts]
    y_parts = _swiglu(u3_parts, w2i_ref, w2o_ref, FINAL_PARTS)
    n = tm // FINAL_PARTS
    hf_parts = _row_parts(jnp.concatenate(h_parts, axis=0), FINAL_PARTS)
    for part, (hp, yp) in enumerate(zip(hf_parts, y_parts)):
        hf = hp + (0.5 * mod[8:9]) * yp
        ms = jnp.mean(hf * hf, axis=-1, keepdims=True)
        out_ref[0, part * n:(part + 1) * n, :] = hf * lax.rsqrt(ms + EPS) * gfin_ref[...]


def _merge_ffn2(h1, p, o, lse, mod, gmix, g3, gfin, wg, wpool, pscale, wpb, wab, wout, w2i, w2o):
    batch, seq, _ = h1.shape
    tm = TOKEN_TILE
    tile = lambda w: pl.BlockSpec((1, tm, w), lambda b, i: (b, i, 0))
    halo = pl.BlockSpec((1, POOL_HALO, POOL_WIDTH),
                        lambda b, i: (b, jnp.maximum(i * (tm // POOL_HALO) - 1, 0), 0))
    grp = [pl.BlockSpec((1, d, tm // d, GROUP_WIDTH), lambda b, i: (b, 0, i, 0)) for d in ATTN_DILATIONS]
    vec = _resident((1, D_MODEL))
    return pl.pallas_call(
        _merge_ffn2_kernel,
        grid=(batch, seq // tm),
        in_specs=[
            tile(D_MODEL), tile(POOL_WIDTH), halo,
            *grp, *grp,
            pl.BlockSpec((1, N_MOD, D_MODEL), lambda b, i: (b, 0, 0)),
            vec, vec, vec,
            _resident(wg.shape), _resident(wpool.shape), _resident(pscale.shape),
            _resident(wpb.shape), _resident(wab.shape), _resident(wout.shape),
            _resident(w2i.shape), _resident(w2o.shape),
        ],
        out_specs=tile(D_MODEL),
        out_shape=jax.ShapeDtypeStruct((batch, seq, D_MODEL), F32),
        scratch_shapes=[pltpu.VMEM((2 * SLABS, tm, LANES), F32),
                        pltpu.VMEM((2 * SLABS, ATTN_DILATIONS[2] * WIDE_PITCH, LANES), F32),
                        pltpu.VMEM((tm + POOL_HALO, POOL_WIDTH), F32),
                        pltpu.VMEM((tm, POOL_WIDTH), BF16),
                        pltpu.VMEM((tm, GROUP_WIDTH), BF16)],
        compiler_params=pltpu.CompilerParams(
            dimension_semantics=("parallel", "parallel"),
            vmem_limit_bytes=VMEM_LIMIT_BYTES),
        name="merge_ffn2",
    )(h1, p, p, *o, *lse, mod, gmix, g3, gfin, wg, wpool, pscale, wpb, wab, wout, w2i, w2o)


def kernel(x, c, positions, w_ada, b_ada, g_norm_ffn1, w_ffn1_in, w_ffn1_out, g_norm_mix, w_in, w_pool,
           pool_scale, w_pool_branch, w_attn_branch, w_out, g_norm_ffn2, w_ffn2_in, w_ffn2_out, g_final):
    assert w_ada.shape[0] == 1, "single-layer block"
    batch, seq, d_model = x.shape
    assert d_model == D_MODEL and seq % TOKEN_TILE == 0
    assert all((seq // d) % QBLK == 0 and ATTN_ROWS % min(ATTN_ROWS, seq // d) == 0 for d in ATTN_DILATIONS)

    mod = _ada_modulation(c, w_ada[0], b_ada[0]).reshape(batch, N_MOD, D_MODEL)

    row = lambda g: g.reshape(1, -1)
    bf = lambda w: w.astype(BF16)
    freq = ROPE_THETA ** (-jnp.arange(0, HEAD_DIM, 2, dtype=F32) / HEAD_DIM)
    invf = jnp.tile(freq, LANES // (HEAD_DIM // 2)).reshape(1, LANES)
    pos4 = positions.reshape(batch, seq // TOKEN_TILE, TOKEN_TILE // LANES, LANES)

    col = jnp.arange(w_in.shape[-1])
    q_cols = (col >= POOL_WIDTH) & (col < POOL_WIDTH + ATTN_WIDTH)
    w_mix = bf(w_in[0] * jnp.where(q_cols, HEAD_DIM ** -0.5, 1.0).astype(F32))
    h1, p, *kvq = _ffn1_proj(x, mod, row(g_norm_ffn1[0]), row(g_norm_mix[0]), pos4, invf,
                             bf(w_ffn1_in[0]), bf(w_ffn1_out[0]), w_mix)

    o, lse = zip(*[_dilated_attention(t) for t in kvq])

    w_pool_bd = jnp.zeros((POOL_WIDTH, POOL_WIDTH), F32)
    for gi in range(len(POOL_WINDOWS)):
        sl = slice(gi * POOL_GROUP_DIM, (gi + 1) * POOL_GROUP_DIM)
        w_pool_bd = w_pool_bd.at[sl, sl].set(w_pool[0, gi])

    return _merge_ffn2(h1, p, o, lse, mod, row(g_norm_mix[0]), row(g_norm_ffn2[0]), row(g_final),
                       w_mix, bf(w_pool_bd), row(pool_scale[0]),
                       bf(w_pool_branch[0]), bf(w_attn_branch[0]), bf(w_out[0]),
                       bf(w_ffn2_in[0]), bf(w_ffn2_out[0]))
```

```python
import jax
import jax.numpy as jnp
from jax import lax
from jax.experimental import pallas as pl
from jax.experimental.pallas import tpu as pltpu

F32 = jnp.float32
BF16 = jnp.bfloat16

D_MODEL = 1024
POOL_WINDOWS = (2, 4, 8, 16)
POOL_GROUP_DIM = 64
POOL_WIDTH = 256
ATTN_DILATIONS = (1, 4, 16)
N_GROUPS = 3
HEAD_DIM = 64
GROUP_WIDTH = 256
ATTN_WIDTH = N_GROUPS * GROUP_WIDTH
PQKV_WIDTH = POOL_WIDTH + 3 * ATTN_WIDTH
KVQ_WIDTH = 3 * GROUP_WIDTH
D_FF = 2816
N_MOD = 9
ROPE_THETA = 10000.0
EPS = 1e-6
NEG_BIG = -1e30

LANES = 128
VMEM_LIMIT_BYTES = 60 * 1024 * 1024

ADA_ROWS = 128
TOKEN_TILE = 512
FF_CHUNKS = ((0, 1024), (1024, 2048), (2048, D_FF))
ATTN_ROWS = 2048
QBLK = 128
POOL_HALO = 16
SLABS = GROUP_WIDTH // LANES
ROW_PARTS = 2
GATE_CHUNKS = 2
FIRST_PARTS = 4
FINAL_PARTS = 4
STAGE_PITCH = 24
WIDE_PITCH = 40
VEC_ROWS = 64


def _resident(shape):
    nd = len(shape)
    return pl.BlockSpec(shape, lambda *_: (0,) * nd, pipeline_mode=pl.Buffered(1))


def _rms_modulate(x, g, shift, scale):
    ms = jnp.mean(x * x, axis=-1, keepdims=True)
    return x * lax.rsqrt(ms + EPS) * (g * (1.0 + scale)) + shift


def _dot(a, b):
    return jnp.dot(a, b, preferred_element_type=F32)


def _sigmoid(x):
    return 0.5 * jnp.tanh(0.5 * x) + 0.5


def _row_parts(x, parts):
    n = x.shape[0] // parts
    return [x[i * n:(i + 1) * n] for i in range(parts)]


def _dot_parts(parts, w):
    return jnp.concatenate([_dot(part, w) for part in parts], axis=0)


def _swiglu(u_parts, w_in_ref, w_out_ref, out_parts):
    u = jnp.concatenate(u_parts, axis=0)

    def up(idx):
        c0, c1 = FF_CHUNKS[idx]
        wa, wb = w_in_ref[:, c0:c1], w_in_ref[:, D_FF + c0:D_FF + c1]
        if idx == 0:
            return _dot_parts(u_parts, wa), _dot_parts(u_parts, wb)
        return _dot(u, wa), _dot(u, wb)

    y = None
    pending = up(0)
    last = len(FF_CHUNKS) - 1
    for idx, (c0, c1) in enumerate(FF_CHUNKS):
        following = up(idx + 1) if idx < last else None
        a, b = pending
        act = (a * _sigmoid(a) * b).astype(BF16)
        if idx == last:
            return [yp + _dot(ap, w_out_ref[c0:c1, :])
                    for yp, ap in zip(_row_parts(y, out_parts), _row_parts(act, out_parts))]
        part = _dot(act, w_out_ref[c0:c1, :])
        y = part if y is None else y + part
        pending = following


def _ada_kernel(c_ref, w_ref, b_ref, o_ref):
    k = pl.program_id(0)
    c = c_ref[...]
    cond = c * jax.nn.sigmoid(c)
    c_hi = cond.astype(BF16)
    c_lo = (cond - c_hi.astype(F32)).astype(BF16)
    w = w_ref[...]
    w_hi = w.astype(BF16)
    w_lo = (w - w_hi.astype(F32)).astype(BF16)
    acc = jnp.dot(c_hi, w_hi, preferred_element_type=F32)
    acc += jnp.dot(c_hi, w_lo, preferred_element_type=F32)
    acc += jnp.dot(c_lo, w_hi, preferred_element_type=F32)

    @pl.when(k == 0)
    def _():
        o_ref[...] = acc + b_ref[...]

    @pl.when(k > 0)
    def _():
        o_ref[...] += acc


def _ada_modulation(c, w_ada, b_ada):
    batch = c.shape[0]
    width = N_MOD * D_MODEL
    return pl.pallas_call(
        _ada_kernel,
        grid=(D_MODEL // ADA_ROWS,),
        in_specs=[
            pl.BlockSpec((batch, ADA_ROWS), lambda k: (0, k)),
            pl.BlockSpec((ADA_ROWS, width), lambda k: (k, 0)),
            pl.BlockSpec((1, width), lambda k: (0, 0)),
        ],
        out_specs=pl.BlockSpec((batch, width), lambda k: (0, 0)),
        out_shape=jax.ShapeDtypeStruct((batch, width), F32),
        compiler_params=pltpu.CompilerParams(dimension_semantics=("arbitrary",)),
        name="ada_modulation",
    )(c, w_ada, b_ada.reshape(1, width))


def _ffn1_proj_kernel(x_ref, mod_ref, g1_ref, g2_ref, pos_ref, invf_ref,
                      w1i_ref, w1o_ref, wp_ref,
                      h_ref, p_ref, kvq0_ref, kvq1_ref, kvq2_ref, trig_ref, stage_ref, wide_ref):
    tm = x_ref.shape[1]
    n_chunks = tm // QBLK
    half = HEAD_DIM // 2
    assert n_chunks * half == LANES

    lane = lax.broadcasted_iota(jnp.int32, (QBLK, LANES), 1)
    quarter = lane >> 5
    first_half = (lane & (HEAD_DIM - 1)) < half
    pos_c = None
    for j in range(n_chunks):
        pos_row = pos_ref[0, 0, j:j + 1, :].astype(F32)
        pos_col = jnp.broadcast_to(pos_row, (QBLK, LANES)).T
        pos_c = pos_col if j == 0 else jnp.where(quarter == j, pos_col, pos_c)
    ang = pos_c * invf_ref[...]
    for t, tbl in enumerate((jnp.cos(ang), jnp.sin(ang))):
        rolled = [tbl] + [pltpu.roll(tbl, half * k, 1) for k in range(1, n_chunks)]
        for j in range(n_chunks):
            full = rolled[(-j) % n_chunks]
            for ql in range(1, n_chunks):
                full = jnp.where(quarter == ql, rolled[(ql - j) % n_chunks], full)
            if t == 1:
                full = jnp.where(first_half, -full, full)
            trig_ref[t, j * QBLK:(j + 1) * QBLK, :] = full

    x = x_ref[0]
    mod = mod_ref[0]
    u_parts = [_rms_modulate(xp, g1_ref[...], mod[0:1], mod[1:2]).astype(BF16) for xp in _row_parts(x, FIRST_PARTS)]
    h = x + (0.5 * mod[2:3]) * _swiglu(u_parts, w1i_ref, w1o_ref, 1)[0]
    h_ref[0] = h
    u2 = _rms_modulate(h, g2_ref[...], mod[3:4], mod[4:5]).astype(BF16)

    qbase, kbase, vbase = POOL_WIDTH, POOL_WIDTH + ATTN_WIDTH, POOL_WIDTH + 2 * ATTN_WIDTH

    def emit(val, which, g, sl):
        if g == 0:
            kvq0_ref[0, 0, :, which * GROUP_WIDTH + sl * LANES:which * GROUP_WIDTH + (sl + 1) * LANES] = val.astype(BF16)
        elif g == 1:
            stage_ref[which * SLABS + sl] = val
        else:
            d = ATTN_DILATIONS[g]
            for m in range(tm // d):
                wide_ref[which * SLABS + sl, m * STAGE_PITCH:m * STAGE_PITCH + d, :] = val[m * d:(m + 1) * d]

    def rotary(t):
        rot = jnp.where(jnp.tile(first_half, (n_chunks, 1)), pltpu.roll(t, LANES - half, 1), pltpu.roll(t, half, 1))
        return t * trig_ref[0] + rot * trig_ref[1]

    pk = _dot(u2, wp_ref[:, kbase:kbase + ATTN_WIDTH])
    pq = _dot(u2, wp_ref[:, qbase:qbase + ATTN_WIDTH])
    for s in range(N_GROUPS * SLABS):
        emit(rotary(pk[:, s * LANES:(s + 1) * LANES]), 0, s // SLABS, s % SLABS)
    pv = _dot(u2, wp_ref[:, vbase:vbase + ATTN_WIDTH])
    for s in range(N_GROUPS * SLABS):
        emit(rotary(pq[:, s * LANES:(s + 1) * LANES]), 2, s // SLABS, s % SLABS)
    pp = _dot(u2, wp_ref[:, :POOL_WIDTH])
    for s in range(N_GROUPS * SLABS):
        emit(pv[:, s * LANES:(s + 1) * LANES], 1, s // SLABS, s % SLABS)

    d1, d2 = ATTN_DILATIONS[1], ATTN_DILATIONS[2]
    for which in range(3):
        for sl in range(SLABS):
            cols = slice(which * GROUP_WIDTH + sl * LANES, which * GROUP_WIDTH + (sl + 1) * LANES)
            for r in range(d1):
                kvq1_ref[0, r, :, cols] = stage_ref[which * SLABS + sl, pl.ds(r, tm // d1, stride=d1), :].astype(BF16)
            for r in range(d2):
                kvq2_ref[0, r, :, cols] = \
                    wide_ref[which * SLABS + sl, pl.ds(r, tm // d2, stride=STAGE_PITCH), :].astype(BF16)
    p_ref[0] = pp


def _ffn1_proj(x, mod, g1, g2, pos4, invf, w1i, w1o, wp):
    batch, seq, _ = x.shape
    tm = TOKEN_TILE
    tile = lambda w: pl.BlockSpec((1, tm, w), lambda b, i: (b, i, 0))
    kvq_specs = [pl.BlockSpec((1, d, tm // d, KVQ_WIDTH), lambda b, i: (b, 0, i, 0)) for d in ATTN_DILATIONS]
    kvq_shapes = [jax.ShapeDtypeStruct((batch, d, seq // d, KVQ_WIDTH), BF16) for d in ATTN_DILATIONS]
    return pl.pallas_call(
        _ffn1_proj_kernel,
        grid=(batch, seq // tm),
        in_specs=[
            tile(D_MODEL),
            pl.BlockSpec((1, N_MOD, D_MODEL), lambda b, i: (b, 0, 0)),
            _resident((1, D_MODEL)),
            _resident((1, D_MODEL)),
            pl.BlockSpec((1, 1, tm // LANES, LANES), lambda b, i: (b, i, 0, 0)),
            _resident((1, LANES)),
            _resident(w1i.shape),
            _resident(w1o.shape),
            _resident(wp.shape),
        ],
        out_specs=[tile(D_MODEL), tile(POOL_WIDTH)] + kvq_specs,
        out_shape=[
            jax.ShapeDtypeStruct((batch, seq, D_MODEL), F32),
            jax.ShapeDtypeStruct((batch, seq, POOL_WIDTH), F32),
        ] + kvq_shapes,
        scratch_shapes=[pltpu.VMEM((2, tm, LANES), F32),
                        pltpu.VMEM((3 * SLABS, tm, LANES), F32),
                        pltpu.VMEM((3 * SLABS, tm // ATTN_DILATIONS[2] * STAGE_PITCH, LANES), F32)],
        compiler_params=pltpu.CompilerParams(
            dimension_semantics=("parallel", "parallel"),
            vmem_limit_bytes=VMEM_LIMIT_BYTES),
        name="ffn1_proj",
    )(x, mod, g1, g2, pos4, invf, w1i, w1o, wp)


def _attn_kernel(cur_ref, prev_ref, o_ref, lse_ref):
    n_res = cur_ref.shape[1]
    nq = cur_ref.shape[2] // QBLK
    has_prev = pl.program_id(2) > 0

    a = lax.broadcasted_iota(jnp.int32, (2 * QBLK, 2 * QBLK), 0) & (QBLK - 1)
    c = lax.broadcasted_iota(jnp.int32, (2 * QBLK, 2 * QBLK), 1)
    bias = jnp.where((c >= a) & (c <= a + QBLK), 0.0, NEG_BIG).astype(F32)
    bias_first = jnp.where((c >= QBLK) | has_prev, bias, NEG_BIG)
    head0 = lax.broadcasted_iota(jnp.int32, (QBLK, LANES), 1) < HEAD_DIM
    ones = jnp.ones((2 * QBLK, LANES), BF16)
    dn = (((1,), (1,)), ((), ()))

    def band(r, j, cols):
        if j == 0:
            return jnp.concatenate([prev_ref[0, r, :, cols], cur_ref[0, r, :QBLK, cols]], axis=0)
        return cur_ref[0, r, (j - 1) * QBLK:(j + 1) * QBLK, cols]

    blocks = [(r, sl, j) for r in range(n_res) for sl in range(SLABS) for j in range(nq)]
    bias_h, bias_first_h = bias[:QBLK], bias_first[:QBLK]
    scores = []
    for r, sl, j in blocks:
        qs = cur_ref[0, r, j * QBLK:(j + 1) * QBLK, 2 * GROUP_WIDTH + sl * LANES:2 * GROUP_WIDTH + (sl + 1) * LANES]
        zero = jnp.zeros_like(qs)
        kb = band(r, j, slice(sl * LANES, (sl + 1) * LANES))
        for hh in range(2):
            lhs = jnp.where(head0, qs, zero) if hh == 0 else jnp.where(head0, zero, qs)
            s = lax.dot_general(lhs, kb, dn, preferred_element_type=F32)
            scores.append(s + (bias_first_h if j == 0 else bias_h))
    probs = []
    for s in scores:
        m = jnp.max(s, axis=-1, keepdims=True)
        probs.append((jnp.exp(s - m).astype(BF16), m))
    for idx, (r, sl, j) in enumerate(blocks):
        vband = band(r, j, slice(GROUP_WIDTH + sl * LANES, GROUP_WIDTH + (sl + 1) * LANES))
        rhs = jnp.concatenate([vband, ones], axis=1)
        (p0, m0), (p1, m1) = probs[2 * idx], probs[2 * idx + 1]
        out0 = jnp.dot(p0, rhs, preferred_element_type=F32)
        out1 = jnp.dot(p1, rhs, preferred_element_type=F32)
        num = jnp.where(head0, out0[:, :LANES], out1[:, :LANES])
        den = jnp.where(head0, out0[:, LANES:], out1[:, LANES:])
        mm = jnp.where(head0, jnp.broadcast_to(m0, (QBLK, LANES)), jnp.broadcast_to(m1, (QBLK, LANES)))
        rows = slice(j * QBLK, (j + 1) * QBLK)
        ocols = slice(sl * LANES, (sl + 1) * LANES)
        o_ref[0, r, rows, ocols] = num / den
        lse_ref[0, r, rows, ocols] = mm + jnp.log(den)


def _dilated_attention(kvq):
    batch, d, length, _ = kvq.shape
    n = min(ATTN_ROWS, length)
    n_res = ATTN_ROWS // n
    cur = pl.BlockSpec((1, n_res, n, KVQ_WIDTH), lambda b, r, i: (b, r, i, 0))
    prev = pl.BlockSpec((1, n_res, QBLK, 2 * GROUP_WIDTH),
                        lambda b, r, i: (b, r, jnp.maximum(i * (n // QBLK) - 1, 0), 0))
    out = pl.BlockSpec((1, n_res, n, GROUP_WIDTH), lambda b, r, i: (b, r, i, 0))
    out_shape = jax.ShapeDtypeStruct((batch, d, length, GROUP_WIDTH), F32)
    return pl.pallas_call(
        _attn_kernel,
        grid=(batch, d // n_res, length // n),
        in_specs=[cur, prev],
        out_specs=[out, out],
        out_shape=[out_shape, out_shape],
        compiler_params=pltpu.CompilerParams(
            dimension_semantics=("parallel", "parallel", "parallel"),
            vmem_limit_bytes=VMEM_LIMIT_BYTES),
        name=f"dilated_attn_d{d}",
    )(kvq, kvq)


def _merge_ffn2_kernel(h_ref, p_ref, pprev_ref, o0_ref, o1_ref, o2_ref, l0_ref, l1_ref, l2_ref,
                       mod_ref, gmix_ref, g3_ref, gfin_ref,
                       wg_ref, wpool_ref, pscale_ref, wpb_ref, wab_ref, wout_ref, w2i_ref, w2o_ref,
                       out_ref, tok_ref, wide_ref, xs_ref, dpool_ref, yattn_ref):
    tm = h_ref.shape[1]
    i = pl.program_id(1)
    mod = mod_ref[0]
    h_parts = _row_parts(h_ref[0], ROW_PARTS)
    u_parts = [_rms_modulate(hp, gmix_ref[...], mod[3:4], mod[4:5]).astype(BF16) for hp in h_parts]
    u = jnp.concatenate(u_parts, axis=0)
    z_pool = _dot_parts(u_parts, wg_ref[:, PQKV_WIDTH:PQKV_WIDTH + D_MODEL])

    xs_ref[:POOL_HALO] = jnp.where(i > 0, pprev_ref[0], 0.0)
    xs_ref[POOL_HALO:] = p_ref[0]
    lane = lax.broadcasted_iota(jnp.int32, (VEC_ROWS, POOL_WIDTH), 1)
    row = lax.broadcasted_iota(jnp.int32, (VEC_ROWS, POOL_WIDTH), 0)
    for r0 in range(0, tm, VEC_ROWS):
        sums = {1: xs_ref[r0:r0 + VEC_ROWS + POOL_HALO]}
        w = 1
        while w < POOL_WINDOWS[-1]:
            t = sums[w]
            sums[2 * w] = t[w:] + t[:-w]
            w *= 2
        trail = jnp.zeros((VEC_ROWS, POOL_WIDTH), F32)
        win = jnp.zeros((VEC_ROWS, POOL_WIDTH), jnp.int32)
        for gi, wnd in enumerate(POOL_WINDOWS):
            in_group = (lane >> 6) == gi
            start = POOL_HALO - (wnd - 1)
            trail = jnp.where(in_group, sums[wnd][start:start + VEC_ROWS], trail)
            win = jnp.where(in_group, wnd, win)
        count = jnp.minimum(row + (i * tm + r0 + 1), win).astype(F32)
        dpool_ref[r0:r0 + VEC_ROWS] = (trail / count - sums[1][POOL_HALO:]).astype(BF16)
    y_pool = (_dot(dpool_ref[...], wpool_ref[...]) * pscale_ref[...]).astype(BF16)
    b_pool = _dot(y_pool, wpb_ref[...])

    gw = D_MODEL // GATE_CHUNKS
    gbase = PQKV_WIDTH + D_MODEL
    z_attn = [_dot(u, wg_ref[:, gbase + c * gw:gbase + (c + 1) * gw]) for c in range(GATE_CHUNKS)]

    d1, d2 = ATTN_DILATIONS[1], ATTN_DILATIONS[2]
    for a, ref in enumerate((o1_ref, l1_ref)):
        for sl in range(SLABS):
            for r in range(d1):
                tok_ref[a * SLABS + sl, pl.ds(r, tm // d1, stride=d1), :] = ref[0, r, :, sl * LANES:(sl + 1) * LANES]
    for a, ref in enumerate((o2_ref, l2_ref)):
        for sl in range(SLABS):
            for r in range(d2):
                wide_ref[a * SLABS + sl, r * WIDE_PITCH:r * WIDE_PITCH + tm // d2, :] = \
                    ref[0, r, :, sl * LANES:(sl + 1) * LANES]

    def wide_rows(a, sl, r0):
        return jnp.concatenate([wide_ref[a * SLABS + sl, pl.ds(m, d2, stride=WIDE_PITCH), :]
                                for m in range(r0 // d2, (r0 + VEC_ROWS) // d2)], axis=0)

    for sl in range(SLABS):
        cols = slice(sl * LANES, (sl + 1) * LANES)
        for r0 in range(0, tm, VEC_ROWS):
            rows = slice(r0, r0 + VEC_ROWS)
            o0, l0 = o0_ref[0, 0, rows, cols], l0_ref[0, 0, rows, cols]
            o1, l1 = tok_ref[0 * SLABS + sl, rows, :], tok_ref[1 * SLABS + sl, rows, :]
            o2, l2 = wide_rows(0, sl, r0), wide_rows(1, sl, r0)
            mx = jnp.maximum(jnp.maximum(l0, l1), l2)
            e0, e1, e2 = jnp.exp(l0 - mx), jnp.exp(l1 - mx), jnp.exp(l2 - mx)
            yattn_ref[rows, cols] = ((e0 * o0 + e1 * o1 + e2 * o2) / (e0 + e1 + e2)).astype(BF16)
    y_attn = yattn_ref[...]
    b_attn = _dot(y_attn, wab_ref[...])

    gated_pool = _sigmoid(z_pool) * b_pool
    mix = None
    for c in range(GATE_CHUNKS):
        cols = slice(c * gw, (c + 1) * gw)
        merged = (gated_pool[:, cols] + _sigmoid(z_attn[c]) * b_attn[:, cols]).astype(BF16)
        w_rows = wout_ref[cols, :]
        if c + 1 < GATE_CHUNKS:
            part = _dot(merged, w_rows)
            mix = part if mix is None else mix + part
        else:
            mix_parts = [mp + _dot(gp, w_rows)
                         for mp, gp in zip(_row_parts(mix, ROW_PARTS), _row_parts(merged, ROW_PARTS))]
    h_parts = [hp + mod[5:6] * mp for hp, mp in zip(h_parts, mix_parts)]

    u3_parts = [_rms_modulate(hp, g3_ref[...], mod[6:7], mod[7:8]).astype(BF16) for hp in h_parts]
    y_parts = _swiglu(u3_parts, w2i_ref, w2o_ref, FINAL_PARTS)
    n = tm // FINAL_PARTS
    hf_parts = _row_parts(jnp.concatenate(h_parts, axis=0), FINAL_PARTS)
    for part, (hp, yp) in enumerate(zip(hf_parts, y_parts)):
        hf = hp + (0.5 * mod[8:9]) * yp
        ms = jnp.mean(hf * hf, axis=-1, keepdims=True)
        out_ref[0, part * n:(part + 1) * n, :] = hf * lax.rsqrt(ms + EPS) * gfin_ref[...]


def _merge_ffn2(h1, p, o, lse, mod, gmix, g3, gfin, wg, wpool, pscale, wpb, wab, wout, w2i, w2o):
    batch, seq, _ = h1.shape
    tm = TOKEN_TILE
    tile = lambda w: pl.BlockSpec((1, tm, w), lambda b, i: (b, i, 0))
    halo = pl.BlockSpec((1, POOL_HALO, POOL_WIDTH),
                        lambda b, i: (b, jnp.maximum(i * (tm // POOL_HALO) - 1, 0), 0))
    grp = [pl.BlockSpec((1, d, tm // d, GROUP_WIDTH), lambda b, i: (b, 0, i, 0)) for d in ATTN_DILATIONS]
    vec = _resident((1, D_MODEL))
    return pl.pallas_call(
        _merge_ffn2_kernel,
        grid=(batch, seq // tm),
        in_specs=[
            tile(D_MODEL), tile(POOL_WIDTH), halo,
            *grp, *grp,
            pl.BlockSpec((1, N_MOD, D_MODEL), lambda b, i: (b, 0, 0)),
            vec, vec, vec,
            _resident(wg.shape), _resident(wpool.shape), _resident(pscale.shape),
            _resident(wpb.shape), _resident(wab.shape), _resident(wout.shape),
            _resident(w2i.shape), _resident(w2o.shape),
        ],
        out_specs=tile(D_MODEL),
        out_shape=jax.ShapeDtypeStruct((batch, seq, D_MODEL), F32),
        scratch_shapes=[pltpu.VMEM((2 * SLABS, tm, LANES), F32),
                        pltpu.VMEM((2 * SLABS, ATTN_DILATIONS[2] * WIDE_PITCH, LANES), F32),
                        pltpu.VMEM((tm + POOL_HALO, POOL_WIDTH), F32),
                        pltpu.VMEM((tm, POOL_WIDTH), BF16),
                        pltpu.VMEM((tm, GROUP_WIDTH), BF16)],
        compiler_params=pltpu.CompilerParams(
            dimension_semantics=("parallel", "parallel"),
            vmem_limit_bytes=VMEM_LIMIT_BYTES),
        name="merge_ffn2",
    )(h1, p, p, *o, *lse, mod, gmix, g3, gfin, wg, wpool, pscale, wpb, wab, wout, w2i, w2o)


def kernel(x, c, positions, w_ada, b_ada, g_norm_ffn1, w_ffn1_in, w_ffn1_out, g_norm_mix, w_in, w_pool,
           pool_scale, w_pool_branch, w_attn_branch, w_out, g_norm_ffn2, w_ffn2_in, w_ffn2_out, g_final):
    assert w_ada.shape[0] == 1, "single-layer block"
    batch, seq, d_model = x.shape
    assert d_model == D_MODEL and seq % TOKEN_TILE == 0
    assert all((seq // d) % QBLK == 0 and ATTN_ROWS % min(ATTN_ROWS, seq // d) == 0 for d in ATTN_DILATIONS)

    mod = _ada_modulation(c, w_ada[0], b_ada[0]).reshape(batch, N_MOD, D_MODEL)

    row = lambda g: g.reshape(1, -1)
    bf = lambda w: w.astype(BF16)
    freq = ROPE_THETA ** (-jnp.arange(0, HEAD_DIM, 2, dtype=F32) / HEAD_DIM)
    invf = jnp.tile(freq, LANES // (HEAD_DIM // 2)).reshape(1, LANES)
    pos4 = positions.reshape(batch, seq // TOKEN_TILE, TOKEN_TILE // LANES, LANES)

    col = jnp.arange(w_in.shape[-1])
    q_cols = (col >= POOL_WIDTH) & (col < POOL_WIDTH + ATTN_WIDTH)
    w_mix = bf(w_in[0] * jnp.where(q_cols, HEAD_DIM ** -0.5, 1.0).astype(F32))
    h1, p, *kvq = _ffn1_proj(x, mod, row(g_norm_ffn1[0]), row(g_norm_mix[0]), pos4, invf,
                             bf(w_ffn1_in[0]), bf(w_ffn1_out[0]), w_mix)

    o, lse = zip(*[_dilated_attention(t) for t in kvq])

    w_pool_bd = jnp.zeros((POOL_WIDTH, POOL_WIDTH), F32)
    for gi in range(len(POOL_WINDOWS)):
        sl = slice(gi * POOL_GROUP_DIM, (gi + 1) * POOL_GROUP_DIM)
        w_pool_bd = w_pool_bd.at[sl, sl].set(w_pool[0, gi])

    return _merge_ffn2(h1, p, o, lse, mod, row(g_norm_mix[0]), row(g_norm_ffn2[0]), row(g_final),
                       w_mix, bf(w_pool_bd), row(pool_scale[0]),
                       bf(w_pool_branch[0]), bf(w_attn_branch[0]), bf(w_out[0]),
                       bf(w_ffn2_in[0]), bf(w_ffn2_out[0]))
```
